```python
import math
import jax, jax.numpy as jnp
from jax import lax
import numpy as np

D_MODEL = 1024
BATCH = 4
SEQ = 4096
DEPTH = 2
DEC_BATCH = 32
DEC_SEQ = 8
PAST_LEN = 16384
PAGE_SIZE = 128

GLA_HEADS = 4
GLA_DK = D_MODEL // 2 // GLA_HEADS
GLA_DV = D_MODEL // GLA_HEADS
GLA_RANK = 16
GLA_TAU = 16.0
GLA_CHUNK = 64
DIFF_HEADS = 8
DIFF_KV_HEADS = 4
DIFF_GROUP = DIFF_HEADS // DIFF_KV_HEADS
DIFF_DH = D_MODEL // DIFF_HEADS // 2
ROT_DIM = DIFF_DH // 4
ROPE_THETA = 500000.0
Q_BLOCK = 128
EPS = 1e-6

W_GLA_Q = GLA_HEADS * GLA_DK
W_GLA_K = GLA_HEADS * GLA_DK
W_GLA_V = GLA_HEADS * GLA_DV
W_GLA_G = W_GLA_V
W_DQ = DIFF_HEADS * 2 * DIFF_DH
W_DK = DIFF_KV_HEADS * 2 * DIFF_DH
W_DV = DIFF_KV_HEADS * 2 * DIFF_DH
W_DG = DIFF_HEADS * 2 * DIFF_DH
SPLIT_SIZES = (W_GLA_Q, W_GLA_K, W_GLA_V, W_GLA_G, GLA_RANK, W_DQ, W_DK, W_DV, W_DG, D_MODEL, D_MODEL)
N_IN = W_GLA_Q + W_GLA_K + W_GLA_V + W_GLA_G + GLA_RANK + W_DQ + W_DK + W_DV + W_DG + 2 * D_MODEL

kernel_name = 'hybrid_gla_diffattn_step'


def _split_points():
    return [int(v) for v in np.cumsum(SPLIT_SIZES)[:-1]]


def _rmsnorm(x, g):
    xf = x.astype(jnp.float32)
    y = xf * lax.rsqrt(jnp.mean(xf * xf, axis=-1, keepdims=True) + EPS) * g.astype(jnp.float32)
    return y.astype(x.dtype)


def _rope_partial(x, pos):
    half = ROT_DIM // 2
    inv = ROPE_THETA ** (-jnp.arange(0, ROT_DIM, 2, dtype=jnp.float32) / ROT_DIM)
    ang = pos.astype(jnp.float32)[:, None] * inv[None, :]
    shp = (pos.shape[0],) + (1,) * (x.ndim - 3) + (half,)
    cos = jnp.cos(ang).reshape(shp)
    sin = jnp.sin(ang).reshape(shp)
    xr = x[..., :ROT_DIM].astype(jnp.float32)
    x1, x2 = xr[..., :half], xr[..., half:]
    rot = jnp.concatenate([x1 * cos - x2 * sin, x1 * sin + x2 * cos], axis=-1)
    return jnp.concatenate([rot.astype(x.dtype), x[..., ROT_DIM:]], axis=-1)


def _gla(q, k, v, log_a, s0, chunk):
    B, L, H, DK = q.shape
    DV = v.shape[-1]
    n = L // chunk

    def to_chunks(t):
        return t.astype(jnp.float32).reshape(B, n, chunk, H, t.shape[-1]).transpose(1, 0, 3, 2, 4)

    qc, kc, vc, ac = to_chunks(q), to_chunks(k), to_chunks(v), to_chunks(log_a)
    mask = jnp.tril(jnp.ones((chunk, chunk), bool))[..., None]

    def step(s, inp):
        qi, ki, vi, ai = inp
        b = jnp.cumsum(ai, axis=2)
        diff = b[:, :, :, None, :] - b[:, :, None, :, :]
        decay = jnp.exp(jnp.where(mask, diff, -jnp.inf))
        att = jnp.einsum('bhtd,bhsd,bhtsd->bhts', qi, ki, decay)
        o = jnp.einsum('bhts,bhsv->bhtv', att, vi) + jnp.einsum('bhtd,bhdv->bhtv', qi * jnp.exp(b), s)
        bl = b[:, :, -1:, :]
        s_new = jnp.exp(bl[:, :, 0, :])[..., None] * s + jnp.einsum('bhsd,bhsv->bhdv', ki * jnp.exp(bl - b), vi)
        return s_new, o

    s_fin, o = lax.scan(step, s0, (qc, kc, vc, ac))
    o = o.transpose(1, 0, 3, 2, 4).reshape(B, L, H, DV)
    return o.astype(v.dtype), s_fin


def _diff_core(q, k, v, mask, lam):
    s = jnp.einsum('bqkgmd,bskmd->bkgmqs', q, k).astype(jnp.float32) * (DIFF_DH ** -0.5)
    s = jnp.where(mask, s, -jnp.inf)
    p = jax.nn.softmax(s, axis=-1)
    a = p[:, :, :, 0] - lam * p[:, :, :, 1]
    return jnp.einsum('bkgqs,bskv->bqkgv', a.astype(v.dtype), v)


def _attend_prompt(q, k, v, lam):
    B, L = q.shape[0], q.shape[1]
    qb = min(Q_BLOCK, L)
    n = L // qb
    kpos = jnp.arange(L)

    def blk(i):
        qi = lax.dynamic_slice_in_dim(q, i * qb, qb, axis=1)
        qpos = i * qb + jnp.arange(qb)
        return _diff_core(qi, k, v, kpos[None, :] <= qpos[:, None], lam)

    o = lax.map(blk, jnp.arange(n))
    return jnp.moveaxis(o, 0, 1).reshape((B, L) + o.shape[3:])


def _make_attend_sample(ck, cv, page_table):
    past = page_table.shape[1] * ck.shape[1]

    def attend(q, k, v, lam):
        T = q.shape[1]
        mask = jnp.concatenate([jnp.ones((T, past), bool), jnp.tril(jnp.ones((T, T), bool))], axis=1)

        def one(args):
            qi, ki, vi, pt = args
            kp = ck[pt].reshape(past, DIFF_KV_HEADS, 2, DIFF_DH).astype(ki.dtype)
            vp = cv[pt].reshape(past, DIFF_KV_HEADS, 2 * DIFF_DH).astype(vi.dtype)
            kall = jnp.concatenate([kp, ki], axis=0)[None]
            vall = jnp.concatenate([vp, vi], axis=0)[None]
            return _diff_core(qi[None], kall, vall, mask, lam)[0]

        return lax.map(one, (q, k, v, page_table))

    return attend


def _layer(x, pos, l, wts, gla_s0, attend):
    (w_in, gla_w_lr, gla_b_lr, gla_norm, lam_q1, lam_k1, lam_q2, lam_k2,
     diff_norm, w_pa, w_pb, w_o, g_pre, g_post) = wts
    B, L, _ = x.shape
    xn = _rmsnorm(x, g_pre[l])
    h = xn @ w_in[l]
    gq, gk, gv, gg, glr, dq, dk, dv, dg, ga, gb = jnp.split(h, _split_points(), axis=-1)
    q = gq.reshape(B, L, GLA_HEADS, GLA_DK) * (GLA_DK ** -0.5)
    k = gk.reshape(B, L, GLA_HEADS, GLA_DK)
    v = gv.reshape(B, L, GLA_HEADS, GLA_DV)
    log_a = jax.nn.log_sigmoid((glr @ gla_w_lr[l] + gla_b_lr[l]).astype(jnp.float32)) / GLA_TAU
    log_a = log_a.reshape(B, L, GLA_HEADS, GLA_DK)
    o_a, s_fin = _gla(q, k, v, log_a, gla_s0, math.gcd(L, GLA_CHUNK))
    o_a = _rmsnorm(o_a, gla_norm[l]).reshape(B, L, W_GLA_V) * jax.nn.silu(gg)
    qd = _rope_partial(dq.reshape(B, L, DIFF_KV_HEADS, DIFF_GROUP, 2, DIFF_DH), pos)
    kd = _rope_partial(dk.reshape(B, L, DIFF_KV_HEADS, 2, DIFF_DH), pos)
    vd = dv.reshape(B, L, DIFF_KV_HEADS, 2 * DIFF_DH)
    lam_init = 0.8 - 0.6 * math.exp(-0.3 * l)
    lam = (jnp.exp(jnp.sum(lam_q1[l].astype(jnp.float32) * lam_k1[l].astype(jnp.float32)))
           - jnp.exp(jnp.sum(lam_q2[l].astype(jnp.float32) * lam_k2[l].astype(jnp.float32))) + lam_init)
    o_b = attend(qd, kd, vd, lam)
    o_b = _rmsnorm(o_b, diff_norm[l]) * (1.0 - lam_init)
    o_b = o_b.reshape(B, L, W_DG) * jax.nn.silu(dg)
    merged = jax.nn.sigmoid(ga) * (o_a @ w_pa[l]) + jax.nn.sigmoid(gb) * (o_b @ w_pb[l])
    y = _rmsnorm(merged @ w_o[l], g_post[l])
    return x + y, kd.reshape(B, L, DIFF_KV_HEADS, 2 * DIFF_DH), vd, s_fin.astype(x.dtype)


def setup_inputs(seed: int = 0) -> dict:
    key = jax.random.key(seed)
    ks = jax.random.split(key, 24)
    f32 = jnp.float32
    nrm = jax.random.normal
    n_pages = PAST_LEN // PAGE_SIZE
    n_used = DEC_BATCH * n_pages
    n_pool = n_used + n_used // 4
    return {
        'x_prompt': nrm(ks[0], (BATCH, SEQ, D_MODEL), f32),
        'x_sample': nrm(ks[1], (DEC_BATCH, DEC_SEQ, D_MODEL), f32),
        'cache_k': nrm(ks[2], (DEPTH, n_pool, PAGE_SIZE, DIFF_KV_HEADS, 2 * DIFF_DH), f32),
        'cache_v': nrm(ks[3], (DEPTH, n_pool, PAGE_SIZE, DIFF_KV_HEADS, 2 * DIFF_DH), f32),
        'state_gla': 2.0 * nrm(ks[4], (DEPTH, DEC_BATCH, GLA_HEADS, GLA_DK, GLA_DV), f32),
        'page_table': jax.random.permutation(ks[5], n_pool)[:n_used].reshape(DEC_BATCH, n_pages).astype(jnp.int32),
        'w_in': nrm(ks[6], (DEPTH, D_MODEL, N_IN), f32) * (D_MODEL ** -0.5),
        'gla_w_lr': nrm(ks[7], (DEPTH, GLA_RANK, W_GLA_K), f32) * (GLA_RANK ** -0.5),
        'gla_b_lr': 0.1 * nrm(ks[8], (DEPTH, W_GLA_K), f32),
        'gla_norm': 1.0 + 0.02 * nrm(ks[9], (DEPTH, GLA_DV), f32),
        'lam_q1': 0.1 * nrm(ks[10], (DEPTH, DIFF_DH), f32),
        'lam_k1': 0.1 * nrm(ks[11], (DEPTH, DIFF_DH), f32),
        'lam_q2': 0.1 * nrm(ks[12], (DEPTH, DIFF_DH), f32),
        'lam_k2': 0.1 * nrm(ks[13], (DEPTH, DIFF_DH), f32),
        'diff_norm': 1.0 + 0.02 * nrm(ks[14], (DEPTH, 2 * DIFF_DH), f32),
        'w_pa': nrm(ks[15], (DEPTH, W_GLA_V, D_MODEL), f32) * (W_GLA_V ** -0.5),
        'w_pb': nrm(ks[16], (DEPTH, W_DG, D_MODEL), f32) * (W_DG ** -0.5),
        'w_o': nrm(ks[17], (DEPTH, D_MODEL, D_MODEL), f32) * (D_MODEL ** -0.5),
        'g_pre': 1.0 + 0.02 * nrm(ks[18], (DEPTH, D_MODEL), f32),
        'g_post': 1.0 + 0.02 * nrm(ks[19], (DEPTH, D_MODEL), f32),
    }


def reference(x_prompt, x_sample, cache_k, cache_v, state_gla, page_table, w_in, gla_w_lr, gla_b_lr,
              gla_norm, lam_q1, lam_k1, lam_q2, lam_k2, diff_norm, w_pa, w_pb, w_o, g_pre, g_post):
    wts = (w_in, gla_w_lr, gla_b_lr, gla_norm, lam_q1, lam_k1, lam_q2, lam_k2,
           diff_norm, w_pa, w_pb, w_o, g_pre, g_post)
    bp, lp = x_prompt.shape[0], x_prompt.shape[1]
    ls = x_sample.shape[1]
    past = page_table.shape[1] * cache_k.shape[2]
    pos_p = jnp.arange(lp, dtype=jnp.int32)
    pos_s = past + jnp.arange(ls, dtype=jnp.int32)
    xp, xs = x_prompt, x_sample
    kp_l, vp_l, sp_l, ks_l, vs_l, ss_l = [], [], [], [], [], []
    for l in range(DEPTH):
        s0 = jnp.zeros((bp, GLA_HEADS, GLA_DK, GLA_DV), jnp.float32)
        xp, kp, vp, sp = _layer(xp, pos_p, l, wts, s0, _attend_prompt)
        xs, kss, vss, sss = _layer(xs, pos_s, l, wts, state_gla[l].astype(jnp.float32),
                                   _make_attend_sample(cache_k[l], cache_v[l], page_table))
        kp_l.append(kp); vp_l.append(vp); sp_l.append(sp)
        ks_l.append(kss); vs_l.append(vss); ss_l.append(sss)
    return (xp, xs, jnp.stack(kp_l), jnp.stack(vp_l), jnp.stack(sp_l),
            jnp.stack(ks_l), jnp.stack(vs_l), jnp.stack(ss_l))
```

```python
import functools
import math

import jax
import jax.numpy as jnp
import numpy as np
from jax import lax
from jax.experimental import pallas as pl
from jax.experimental.pallas import tpu as pltpu

F32 = jnp.float32
BF16 = jnp.bfloat16

D_MODEL = 1024
GLA_HEADS = 4
GLA_DK = 128
GLA_DV = 256
GLA_RANK = 16
GLA_TAU = 16.0
GLA_CHUNK = 64
DIFF_HEADS = 8
DIFF_KV_HEADS = 4
DIFF_GROUP = 2
DIFF_DH = 64
ROT_DIM = 16
ROPE_THETA = 500000.0
EPS = 1e-6
LANES = 128
VMEM_LIMIT = 56 * 1024 * 1024

W_GLA_Q = GLA_HEADS * GLA_DK
W_GLA_V = GLA_HEADS * GLA_DV
W_DQ = DIFF_HEADS * 2 * DIFF_DH
W_DK = DIFF_KV_HEADS * 2 * DIFF_DH

C_GQ = 0
C_GK = C_GQ + W_GLA_Q
C_GV = C_GK + W_GLA_Q
C_GG = C_GV + W_GLA_V
C_DQ = C_GG + W_GLA_V
C_DK = C_DQ + W_DQ
C_DV = C_DK + W_DK
C_DG = C_DV + W_DK
C_GA = C_DG + W_DQ
C_GB = C_GA + D_MODEL
C_LR = C_GB + D_MODEL
N_PROJ = C_LR + LANES


def _silu(x):
    return x * jax.nn.sigmoid(x)


def _dot(a, b):
    return jnp.dot(a, b, preferred_element_type=F32)


def _dot_nt(a, b):
    return lax.dot_general(a, b, (((1,), (1,)), ((), ())), preferred_element_type=F32)


def _dot_tn(a, b, precision=None):
    return lax.dot_general(a, b, (((0,), (0,)), ((), ())), preferred_element_type=F32,
                           precision=precision)


def _inproj_kernel(x_ref, gpre_ref, w_ref, wlr_ref, blr_ref, cos_ref, sup_ref, sdn_ref,
                   gq_ref, gk_ref, gv_ref, gg_ref, la_ref, qh_ref, kf_ref, kb_ref,
                   vf_ref, vb_ref, dg_ref, ga_ref, gb_ref):
    x = x_ref[...]
    ms = jnp.mean(x * x, axis=-1, keepdims=True)
    xn = (x * lax.rsqrt(ms + EPS) * gpre_ref[...]).astype(BF16)

    def proj(c0, width):
        return _dot(xn, w_ref[:, c0:c0 + width])

    gq_ref[...] = (proj(C_GQ, W_GLA_Q) * (GLA_DK ** -0.5)).astype(gq_ref.dtype)
    gk_ref[...] = proj(C_GK, W_GLA_Q).astype(gk_ref.dtype)
    gv_ref[...] = proj(C_GV, W_GLA_V).astype(gv_ref.dtype)
    gg_ref[...] = _silu(proj(C_GG, W_GLA_V)).astype(gg_ref.dtype)
    dg_ref[...] = _silu(proj(C_DG, W_DQ)).astype(dg_ref.dtype)
    ga_ref[...] = jax.nn.sigmoid(proj(C_GA, D_MODEL)).astype(ga_ref.dtype)
    gb_ref[...] = jax.nn.sigmoid(proj(C_GB, D_MODEL)).astype(gb_ref.dtype)

    glr = proj(C_LR, LANES).astype(BF16)
    z = _dot(glr, wlr_ref[...]) + blr_ref[...]
    la_ref[...] = (jnp.minimum(z, 0.0) - jnp.log(1.0 + jnp.exp(-jnp.abs(z)))) * (1.0 / GLA_TAU)

    cos = cos_ref[...]
    sup = sup_ref[...]
    sdn = sdn_ref[...]

    def rope(s):
        return s * cos + pltpu.roll(s, 8, 1) * sup + pltpu.roll(s, LANES - 8, 1) * sdn

    lane = lax.broadcasted_iota(jnp.int32, (1, LANES), 1)
    first = lane < DIFF_DH
    hq = proj(C_DQ, W_DQ)
    for h in range(DIFF_HEADS):
        s = rope(hq[:, h * LANES:(h + 1) * LANES]) * (DIFF_DH ** -0.5)
        kv, g = divmod(h, DIFF_GROUP)
        qh_ref[kv * 4 + g] = jnp.where(first, s, 0.0).astype(qh_ref.dtype)
        qh_ref[kv * 4 + 2 + g] = jnp.where(first, 0.0, s).astype(qh_ref.dtype)
    hk = proj(C_DK, W_DK)
    for j in range(DIFF_KV_HEADS):
        s = rope(hk[:, j * LANES:(j + 1) * LANES])
        kf_ref[:, j * LANES:(j + 1) * LANES] = s
        kb_ref[:, j * LANES:(j + 1) * LANES] = s.astype(kb_ref.dtype)
    hv = proj(C_DV, W_DK)
    vf_ref[...] = hv
    vb_ref[...] = hv.astype(vb_ref.dtype)


def _inproj(x2d, gpre, w, wlr, blr, tabs, tm, act_dtype):
    T = x2d.shape[0]
    n_tab = tabs[0].shape[0] // tm
    row = lambda width: pl.BlockSpec((tm, width), lambda i: (i, 0))
    const = lambda shape: pl.BlockSpec(shape, lambda i: (0,) * len(shape))
    tab = pl.BlockSpec((tm, LANES), lambda i: (i % n_tab, 0))
    out_shape = [
        jax.ShapeDtypeStruct((T, W_GLA_Q), act_dtype),
        jax.ShapeDtypeStruct((T, W_GLA_Q), act_dtype),
        jax.ShapeDtypeStruct((T, W_GLA_V), act_dtype),
        jax.ShapeDtypeStruct((T, W_GLA_V), act_dtype),
        jax.ShapeDtypeStruct((T, W_GLA_Q), F32),
        jax.ShapeDtypeStruct((16, T, LANES), act_dtype),
        jax.ShapeDtypeStruct((T, W_DK), F32),
        jax.ShapeDtypeStruct((T, W_DK), act_dtype),
        jax.ShapeDtypeStruct((T, W_DK), F32),
        jax.ShapeDtypeStruct((T, W_DK), act_dtype),
        jax.ShapeDtypeStruct((T, W_DQ), act_dtype),
        jax.ShapeDtypeStruct((T, D_MODEL), act_dtype),
        jax.ShapeDtypeStruct((T, D_MODEL), act_dtype),
    ]
    out_specs = [row(W_GLA_Q), row(W_GLA_Q), row(W_GLA_V), row(W_GLA_V), row(W_GLA_Q),
                 pl.BlockSpec((16, tm, LANES), lambda i: (0, i, 0)),
                 row(W_DK), row(W_DK), row(W_DK), row(W_DK), row(W_DQ), row(D_MODEL), row(D_MODEL)]
    return pl.pallas_call(
        _inproj_kernel,
        out_shape=out_shape,
        grid=(T // tm,),
        in_specs=[row(D_MODEL), const((1, D_MODEL)),
                  pl.BlockSpec((D_MODEL, N_PROJ), lambda i: (0, 0), pipeline_mode=pl.Buffered(1)),
                  const((LANES, W_GLA_Q)), const((1, W_GLA_Q)), tab, tab, tab],
        out_specs=out_specs,
        compiler_params=pltpu.CompilerParams(dimension_semantics=("arbitrary",),
                                             vmem_limit_bytes=VMEM_LIMIT),
        name="inproj",
    )(x2d, gpre, w, wlr, blr, *tabs)


def _gla_kernel(*refs, chunk, n_chunks, has_s0, mxu_dtype):
    if has_s0:
        q_ref, k_ref, v_ref, la_ref, gg_ref, norm_ref, s0_ref, o_ref, sfin_ref, s_scr = refs
    else:
        q_ref, k_ref, v_ref, la_ref, gg_ref, norm_ref, o_ref, sfin_ref, s_scr = refs
    i = pl.program_id(2)

    @pl.when(i == 0)
    def _():
        if has_s0:
            s_scr[...] = s0_ref[0, 0]
        else:
            s_scr[...] = jnp.zeros_like(s_scr)

    r = lax.broadcasted_iota(jnp.int32, (chunk, chunk), 0)
    c = lax.broadcasted_iota(jnp.int32, (chunk, chunk), 1)
    causal = r >= c
    tril = causal.astype(F32)
    ones = jnp.ones((chunk, LANES), F32)
    norm = norm_ref[...]

    def body(ci, carry):
        sl = pl.ds(pl.multiple_of(ci * chunk, chunk), chunk)
        la = la_ref[sl, :]
        q = q_ref[sl, :].astype(F32)
        k = k_ref[sl, :].astype(F32)
        v = v_ref[sl, :].astype(mxu_dtype)
        b = jnp.dot(tril, la, preferred_element_type=F32, precision=lax.Precision.HIGHEST)
        bl = b[chunk - 1:chunk, :]
        qt = (q * jnp.exp(b)).astype(mxu_dtype)
        kt = (k * jnp.exp(-b)).astype(mxu_dtype)
        att = jnp.where(causal, _dot_nt(qt, kt), 0.0)
        s_old = s_scr[...]
        o = _dot(att.astype(mxu_dtype), v) + _dot(qt, s_old.astype(mxu_dtype))
        dcol = _dot_tn(la, ones, precision=lax.Precision.HIGHEST)
        dec = jnp.exp(dcol)
        kd = (k * jnp.exp(bl - b)).astype(mxu_dtype)
        s_scr[...] = jnp.concatenate([dec, dec], axis=1) * s_old + _dot_tn(kd, v)
        on = o * lax.rsqrt(jnp.mean(o * o, axis=-1, keepdims=True) + EPS) * norm
        o_ref[sl, :] = (on * gg_ref[sl, :].astype(F32)).astype(o_ref.dtype)
        return carry

    if n_chunks == 1:
        body(0, 0)
    else:
        lax.fori_loop(0, n_chunks, body, 0)

    @pl.when(i == pl.num_programs(2) - 1)
    def _():
        sfin_ref[0, 0] = s_scr[...]


def _gla(gq, gk, gv, la, gg, norm, s0, B, L, lb, chunk, act_dtype, mxu_dtype):
    nl = L // lb
    tok = lambda width: pl.BlockSpec((lb, width), lambda b, h, i: (b * nl + i, h))
    state = pl.BlockSpec((1, 1, GLA_DK, GLA_DV), lambda b, h, i: (b, h, 0, 0))
    in_specs = [tok(GLA_DK), tok(GLA_DK), tok(GLA_DV), tok(GLA_DK), tok(GLA_DV),
                pl.BlockSpec((1, GLA_DV), lambda b, h, i: (0, 0))]
    args = [gq, gk, gv, la, gg, norm]
    if s0 is not None:
        in_specs.append(state)
        args.append(s0)
    return pl.pallas_call(
        functools.partial(_gla_kernel, chunk=chunk, n_chunks=lb // chunk, has_s0=s0 is not None,
                          mxu_dtype=mxu_dtype),
        out_shape=[jax.ShapeDtypeStruct((B * L, W_GLA_V), act_dtype),
                   jax.ShapeDtypeStruct((B, GLA_HEADS, GLA_DK, GLA_DV), F32)],
        grid=(B, GLA_HEADS, nl),
        in_specs=in_specs,
        out_specs=[tok(GLA_DV), state],
        scratch_shapes=[pltpu.VMEM((GLA_DK, GLA_DV), F32)],
        compiler_params=pltpu.CompilerParams(
            dimension_semantics=("parallel", "parallel", "arbitrary"), vmem_limit_bytes=VMEM_LIMIT),
        name="gla",
    )(*args)


def _lambda(lam_ref, lam_init):
    lv = lam_ref[...]
    a = jnp.sum(lv[0:1] * lv[1:2], axis=1, keepdims=True)
    b = jnp.sum(lv[2:3] * lv[3:4], axis=1, keepdims=True)
    return jnp.exp(a) - jnp.exp(b) + lam_init


def _online_update(s, v, m_scr, l_scr, acc_scr, rows, mxu_dtype):
    m_old = m_scr[rows, :]
    m_new = jnp.maximum(m_old, jnp.max(s, axis=-1, keepdims=True))
    alpha = jnp.exp(m_old - m_new)
    p = jnp.exp(s - m_new)
    l_scr[rows, :] = alpha * l_scr[rows, :] + jnp.sum(p, axis=-1, keepdims=True)
    acc_scr[rows, :] = alpha * acc_scr[rows, :] + _dot(p.astype(mxu_dtype), v)
    m_scr[rows, :] = m_new


def _diff_finalize(acc, l, lam, norm, lam_init):
    half = acc.shape[0] // 2
    o = acc[:half] / l[:half] - lam * (acc[half:] / l[half:])
    return o * lax.rsqrt(jnp.mean(o * o, axis=-1, keepdims=True) + EPS) * norm * (1.0 - lam_init)


def _attn_prompt_kernel(qh_ref, k_ref, v_ref, dg_ref, lam_ref, norm_ref, o_ref,
                        m_scr, l_scr, acc_scr, *, tq, tk, lam_init):
    qi = pl.program_id(2)
    ki = pl.program_id(3)
    rows_all = pl.ds(0, 4 * tq)

    @pl.when(ki == 0)
    def _():
        m_scr[...] = jnp.full_like(m_scr, -jnp.inf)
        l_scr[...] = jnp.zeros_like(l_scr)
        acc_scr[...] = jnp.zeros_like(acc_scr)

    def step(masked):
        q = qh_ref[...].reshape(4 * tq, LANES)
        s = _dot_nt(q, k_ref[...])
        if masked:
            r = lax.broadcasted_iota(jnp.int32, (4, tq, tk), 1).reshape(4 * tq, tk)
            c = lax.broadcasted_iota(jnp.int32, (4 * tq, tk), 1)
            s = jnp.where(ki * tk + c <= qi * tq + r, s, -jnp.inf)
        _online_update(s, v_ref[...], m_scr, l_scr, acc_scr, rows_all, BF16)

    q_last = qi * tq + tq - 1
    needed = ki * tk <= q_last
    crosses = ki * tk + tk - 1 > qi * tq

    @pl.when(jnp.logical_and(needed, crosses))
    def _():
        step(True)

    @pl.when(jnp.logical_and(needed, jnp.logical_not(crosses)))
    def _():
        step(False)

    @pl.when(ki == pl.num_programs(3) - 1)
    def _():
        lam = _lambda(lam_ref, lam_init)
        o = _diff_finalize(acc_scr[...], l_scr[...], lam, norm_ref[...], lam_init)
        for g in range(DIFF_GROUP):
            gate = dg_ref[:, g * LANES:(g + 1) * LANES].astype(F32)
            o_ref[:, g * LANES:(g + 1) * LANES] = (o[g * tq:(g + 1) * tq] * gate).astype(o_ref.dtype)


def _attn_prompt(qh, kb, vb, dg, lam4, norm, B, L, tq, tk, lam_init):
    nq, nk = L // tq, L // tk

    def kv_map(b, kv, qi, ki):
        return (b * nk + jnp.minimum(ki, (qi * tq + tq - 1) // tk), kv)

    return pl.pallas_call(
        functools.partial(_attn_prompt_kernel, tq=tq, tk=tk, lam_init=lam_init),
        out_shape=jax.ShapeDtypeStruct((B * L, W_DQ), BF16),
        grid=(B, DIFF_KV_HEADS, nq, nk),
        in_specs=[pl.BlockSpec((4, tq, LANES), lambda b, kv, qi, ki: (kv, b * nq + qi, 0)),
                  pl.BlockSpec((tk, LANES), kv_map),
                  pl.BlockSpec((tk, LANES), kv_map),
                  pl.BlockSpec((tq, 2 * LANES), lambda b, kv, qi, ki: (b * nq + qi, kv)),
                  pl.BlockSpec((4, DIFF_DH), lambda b, kv, qi, ki: (0, 0)),
                  pl.BlockSpec((1, LANES), lambda b, kv, qi, ki: (0, 0))],
        out_specs=pl.BlockSpec((tq, 2 * LANES), lambda b, kv, qi, ki: (b * nq + qi, kv)),
        scratch_shapes=[pltpu.VMEM((4 * tq, 1), F32), pltpu.VMEM((4 * tq, 1), F32),
                        pltpu.VMEM((4 * tq, LANES), F32)],
        compiler_params=pltpu.CompilerParams(
            dimension_semantics=("parallel", "parallel", "parallel", "arbitrary"),
            vmem_limit_bytes=VMEM_LIMIT),
        name="attn_prompt",
    )(qh, kb, vb, dg, lam4, norm)


def _attn_sample_kernel(pt_ref, qh_ref, *refs, pages, t_new, lam_init):
    k_refs = refs[:pages]
    v_refs = refs[pages:2 * pages]
    kn_ref, vn_ref, dg_ref, lam_ref, norm_ref, o_ref, m_scr, l_scr, acc_scr = refs[2 * pages:]
    p = pl.program_id(1)
    nrow = 4 * t_new
    page = k_refs[0].shape[0]

    def q_of(kv):
        return qh_ref[kv * 4:(kv + 1) * 4].reshape(nrow, LANES)

    @pl.when(p == 0)
    def _():
        r = lax.broadcasted_iota(jnp.int32, (4, t_new, LANES), 1).reshape(nrow, LANES)
        c = lax.broadcasted_iota(jnp.int32, (nrow, LANES), 1)
        pad = jnp.zeros((LANES - t_new, LANES), F32)
        for kv in range(DIFF_KV_HEADS):
            rows = pl.ds(kv * nrow, nrow)
            cols = slice(kv * LANES, (kv + 1) * LANES)
            kn = jnp.concatenate([kn_ref[:, cols], pad], axis=0)
            vn = jnp.concatenate([vn_ref[:, cols], pad], axis=0)
            s = jnp.where(c <= r, _dot_nt(q_of(kv), kn), -jnp.inf)
            m = jnp.max(s, axis=-1, keepdims=True)
            e = jnp.exp(s - m)
            m_scr[rows, :] = m
            l_scr[rows, :] = jnp.sum(e, axis=-1, keepdims=True)
            acc_scr[rows, :] = _dot(e, vn)

    for kv in range(DIFF_KV_HEADS):
        rows = pl.ds(kv * nrow, nrow)
        cols = slice(kv * LANES, (kv + 1) * LANES)
        q = q_of(kv)
        s = jnp.concatenate([_dot_nt(q, k_refs[j][:, kv, :]) for j in range(pages)], axis=1)
        v = jnp.concatenate([v_refs[j][:, kv, :] for j in range(pages)], axis=0)
        _online_update(s, v, m_scr, l_scr, acc_scr, rows, F32)

    @pl.when(p == pl.num_programs(1) - 1)
    def _():
        lam = _lambda(lam_ref, lam_init)
        for kv in range(DIFF_KV_HEADS):
            rows = pl.ds(kv * nrow, nrow)
            o = _diff_finalize(acc_scr[rows, :], l_scr[rows, :], lam, norm_ref[...], lam_init)
            for g in range(DIFF_GROUP):
                h = kv * DIFF_GROUP + g
                gate = dg_ref[:, h * LANES:(h + 1) * LANES]
                o_ref[:, h * LANES:(h + 1) * LANES] = o[g * t_new:(g + 1) * t_new] * gate


def _attn_sample(page_table, qh, cache_k, cache_v, layer, kn, vn, dg, lam4, norm, t_new, pages, lam_init):
    nb, n_pages = page_table.shape
    page = cache_k.shape[2]
    pt = page_table.reshape(-1)

    def page_spec(j):
        return pl.BlockSpec((None, None, page, DIFF_KV_HEADS, LANES),
                            lambda b, p, pt_ref: (layer, pt_ref[b * n_pages + p * pages + j], 0, 0, 0))

    tokrow = lambda width: pl.BlockSpec((t_new, width), lambda b, p, pt_ref: (b, 0))
    nrow = 4 * t_new * DIFF_KV_HEADS
    grid_spec = pltpu.PrefetchScalarGridSpec(
        num_scalar_prefetch=1,
        grid=(nb, n_pages // pages),
        in_specs=[pl.BlockSpec((16, t_new, LANES), lambda b, p, pt_ref: (0, b, 0))]
                 + [page_spec(j) for j in range(pages)] * 2
                 + [tokrow(W_DK), tokrow(W_DK), tokrow(W_DQ),
                    pl.BlockSpec((4, DIFF_DH), lambda b, p, pt_ref: (0, 0)),
                    pl.BlockSpec((1, LANES), lambda b, p, pt_ref: (0, 0))],
        out_specs=tokrow(W_DQ),
        scratch_shapes=[pltpu.VMEM((nrow, 1), F32), pltpu.VMEM((nrow, 1), F32),
                        pltpu.VMEM((nrow, LANES), F32)],
    )
    return pl.pallas_call(
        functools.partial(_attn_sample_kernel, pages=pages, t_new=t_new, lam_init=lam_init),
        out_shape=jax.ShapeDtypeStruct((nb * t_new, W_DQ), F32),
        grid_spec=grid_spec,
        compiler_params=pltpu.CompilerParams(dimension_semantics=("parallel", "arbitrary"),
                                             vmem_limit_bytes=VMEM_LIMIT),
        name="attn_sample",
    )(pt, qh, *([cache_k] * pages), *([cache_v] * pages), kn, vn, dg, lam4, norm)


def _merge_kernel(x_ref, oa_ref, ob_ref, ga_ref, gb_ref, wpa_ref, wpb_ref, wo_ref, gpost_ref, y_ref):
    pa = _dot(oa_ref[...].astype(BF16), wpa_ref[...])
    pb = _dot(ob_ref[...].astype(BF16), wpb_ref[...])
    merged = ga_ref[...].astype(F32) * pa + gb_ref[...].astype(F32) * pb
    y = _dot(merged.astype(BF16), wo_ref[...])
    yn = y * lax.rsqrt(jnp.mean(y * y, axis=-1, keepdims=True) + EPS) * gpost_ref[...]
    y_ref[...] = x_ref[...] + yn


def _merge(x2d, oa, ob, ga, gb, wpa, wpb, wo, gpost, tm):
    T = x2d.shape[0]
    row = pl.BlockSpec((tm, D_MODEL), lambda i: (i, 0))
    wspec = pl.BlockSpec((D_MODEL, D_MODEL), lambda i: (0, 0))
    return pl.pallas_call(
        _merge_kernel,
        out_shape=jax.ShapeDtypeStruct((T, D_MODEL), F32),
        grid=(T // tm,),
        in_specs=[row, row, row, row, row, wspec, wspec, wspec,
                  pl.BlockSpec((1, D_MODEL), lambda i: (0, 0))],
        out_specs=row,
        compiler_params=pltpu.CompilerParams(dimension_semantics=("parallel",),
                                             vmem_limit_bytes=VMEM_LIMIT),
        name="merge",
    )(x2d, oa, ob, ga, gb, wpa, wpb, wo, gpost)


def _rope_tables(pos):
    half = ROT_DIM // 2
    lane = np.arange(LANES) % DIFF_DH
    inv = ROPE_THETA ** (-(jnp.arange(half, dtype=F32) * 2.0) / ROT_DIM)
    ang = pos.astype(F32)[:, None] * inv[None, :]
    cos_l = jnp.cos(ang)[:, lane % half]
    sin_l = jnp.sin(ang)[:, lane % half]
    rot = jnp.asarray(lane < ROT_DIM)[None, :]
    upper = jnp.asarray((lane >= half) & (lane < ROT_DIM))[None, :]
    lower = jnp.asarray(lane < half)[None, :]
    return (jnp.where(rot, cos_l, 1.0), jnp.where(upper, sin_l, 0.0), jnp.where(lower, -sin_l, 0.0))


def _prep_weights(w_in_l, gla_w_lr_l):
    c_lr = 2 * W_GLA_Q + 2 * W_GLA_V
    w = jnp.concatenate([w_in_l[:, :c_lr], w_in_l[:, c_lr + GLA_RANK:], w_in_l[:, c_lr:c_lr + GLA_RANK],
                         jnp.zeros((D_MODEL, LANES - GLA_RANK), F32)], axis=1).astype(BF16)
    wlr = jnp.concatenate([gla_w_lr_l, jnp.zeros((LANES - GLA_RANK, W_GLA_Q), F32)], axis=0).astype(BF16)
    return w, wlr


def kernel(x_prompt, x_sample, cache_k, cache_v, state_gla, page_table, w_in, gla_w_lr, gla_b_lr,
           gla_norm, lam_q1, lam_k1, lam_q2, lam_k2, diff_norm, w_pa, w_pb, w_o, g_pre, g_post):
    depth = w_in.shape[0]
    bp, lp, _ = x_prompt.shape
    bs, ls, _ = x_sample.shape
    n_pool, page = cache_k.shape[1], cache_k.shape[2]
    past = page_table.shape[1] * page

    tabs_p = _rope_tables(jnp.arange(lp, dtype=jnp.int32))
    tabs_s = _rope_tables(jnp.tile(past + jnp.arange(ls, dtype=jnp.int32), bs))

    xp = x_prompt.reshape(bp * lp, D_MODEL)
    xs = x_sample.reshape(bs * ls, D_MODEL)
    outs = [[] for _ in range(6)]
    for l in range(depth):
        lam_init = 0.8 - 0.6 * math.exp(-0.3 * l)
        w, wlr = _prep_weights(w_in[l], gla_w_lr[l])
        blr = gla_b_lr[l][None, :]
        gpre = g_pre[l][None, :]
        gpost = g_post[l][None, :]
        gnorm = gla_norm[l][None, :]
        dnorm = diff_norm[l][None, :]
        lam4 = jnp.stack([lam_q1[l], lam_k1[l], lam_q2[l], lam_k2[l]])
        wpa, wpb, wo = w_pa[l].astype(BF16), w_pb[l].astype(BF16), w_o[l].astype(BF16)

        gq, gk, gv, gg, la, qh, kf, kb, vf, vb, dg, ga, gb = _inproj(
            xp, gpre, w, wlr, blr, tabs_p, 256, BF16)
        oa, sfin_p = _gla(gq, gk, gv, la, gg, gnorm, None, bp, lp, 512, math.gcd(lp, GLA_CHUNK), BF16, BF16)
        ob = _attn_prompt(qh, kb, vb, dg, lam4, dnorm, bp, lp, 256, 512, lam_init)
        xp = _merge(xp, oa, ob, ga, gb, wpa, wpb, wo, gpost, 512)
        outs[0].append(kf.reshape(bp, lp, DIFF_KV_HEADS, 2 * DIFF_DH))
        outs[1].append(vf.reshape(bp, lp, DIFF_KV_HEADS, 2 * DIFF_DH))
        outs[2].append(sfin_p)

        gq, gk, gv, gg, la, qh, kf, _, vf, _, dg, ga, gb = _inproj(
            xs, gpre, w, wlr, blr, tabs_s, bs * ls, F32)
        oa, sfin_s = _gla(gq, gk, gv, la, gg, gnorm, state_gla[l], bs, ls, ls, math.gcd(ls, GLA_CHUNK),
                          F32, F32)
        ob = _attn_sample(page_table, qh, cache_k, cache_v, l, kf, vf, dg, lam4, dnorm, ls, 4, lam_init)
        xs = _merge(xs, oa, ob, ga, gb, wpa, wpb, wo, gpost, bs * ls)
        outs[3].append(kf.reshape(bs, ls, DIFF_KV_HEADS, 2 * DIFF_DH))
        outs[4].append(vf.reshape(bs, ls, DIFF_KV_HEADS, 2 * DIFF_DH))
        outs[5].append(sfin_s)

    return (xp.reshape(bp, lp, D_MODEL), xs.reshape(bs, ls, D_MODEL),
            jnp.stack(outs[0]), jnp.stack(outs[1]), jnp.stack(outs[2]),
            jnp.stack(outs[3]), jnp.stack(outs[4]), jnp.stack(outs[5]))
```

```python
import functools
import math

import jax
import jax.numpy as jnp
import numpy as np
from jax import lax
from jax.experimental import pallas as pl
from jax.experimental.pallas import tpu as pltpu

F32 = jnp.float32
BF16 = jnp.bfloat16

D_MODEL = 1024
GLA_HEADS = 4
GLA_DK = 128
GLA_DV = 256
GLA_RANK = 16
GLA_TAU = 16.0
GLA_CHUNK = 64
DIFF_HEADS = 8
DIFF_KV_HEADS = 4
DIFF_GROUP = 2
DIFF_DH = 64
ROT_DIM = 16
ROPE_THETA = 500000.0
EPS = 1e-6
LOG2E = math.log2(math.e)
LANES = 128
VMEM_LIMIT = 56 * 1024 * 1024

W_GLA_Q = GLA_HEADS * GLA_DK
W_GLA_V = GLA_HEADS * GLA_DV
W_DQ = DIFF_HEADS * 2 * DIFF_DH
W_DK = DIFF_KV_HEADS * 2 * DIFF_DH
Q_SLABS = 2 * DIFF_GROUP

C_GQ = 0
C_GK = C_GQ + W_GLA_Q
C_GV = C_GK + W_GLA_Q
C_GG = C_GV + W_GLA_V
C_DQ = C_GG + W_GLA_V
C_DK = C_DQ + W_DQ
C_DV = C_DK + W_DK
C_DG = C_DV + W_DK
C_GA = C_DG + W_DQ
C_GB = C_GA + D_MODEL
C_LR = C_GB + D_MODEL
N_PROJ = C_LR + LANES

INPROJ_TM = 256
GLA_BLOCK = 512
ATTN_TQ = 512
ATTN_TK = 512
MERGE_TM = 512
SAMPLE_PAGES_PER_STEP = 16


def _silu(x):
    return x * jax.nn.sigmoid(x)


def _dot(a, b):
    return jnp.dot(a, b, preferred_element_type=F32)


def _dot_nt(a, b):
    return lax.dot_general(a, b, (((1,), (1,)), ((), ())), preferred_element_type=F32)


def _dot_tn(a, b, precision=None):
    return lax.dot_general(a, b, (((0,), (0,)), ((), ())), preferred_element_type=F32,
                           precision=precision)


def _inproj_kernel(x_ref, gpre_ref, w_ref, wlr_ref, blr_ref, cos_ref, sup_ref, sdn_ref,
                   gq_ref, gk_ref, gv_ref, gg_ref, la_ref, qh_ref, kf_ref, kb_ref,
                   vf_ref, vb_ref, dg_ref, ga_ref, gb_ref):
    x = x_ref[...]
    ms = jnp.mean(x * x, axis=-1, keepdims=True)
    xn = (x * lax.rsqrt(ms + EPS) * gpre_ref[...]).astype(BF16)

    def proj(c0, width):
        return _dot(xn, w_ref[:, c0:c0 + width])

    gq_ref[...] = (proj(C_GQ, W_GLA_Q) * (GLA_DK ** -0.5)).astype(gq_ref.dtype)
    gk_ref[...] = proj(C_GK, W_GLA_Q).astype(gk_ref.dtype)
    gv_ref[...] = proj(C_GV, W_GLA_V).astype(gv_ref.dtype)
    gg_ref[...] = _silu(proj(C_GG, W_GLA_V)).astype(gg_ref.dtype)
    dg_ref[...] = _silu(proj(C_DG, W_DQ)).astype(dg_ref.dtype)
    ga_ref[...] = jax.nn.sigmoid(proj(C_GA, D_MODEL)).astype(ga_ref.dtype)
    gb_ref[...] = jax.nn.sigmoid(proj(C_GB, D_MODEL)).astype(gb_ref.dtype)

    glr = proj(C_LR, LANES).astype(BF16)
    z = _dot(glr, wlr_ref[...]) + blr_ref[...]
    la_ref[...] = (jnp.minimum(z, 0.0) - jnp.log(1.0 + jnp.exp(-jnp.abs(z)))) * (1.0 / GLA_TAU)

    cos = cos_ref[...]
    sup = sup_ref[...]
    sdn = sdn_ref[...]

    def rope(s):
        return s * cos + pltpu.roll(s, 8, 1) * sup + pltpu.roll(s, LANES - 8, 1) * sdn

    lane = lax.broadcasted_iota(jnp.int32, (1, LANES), 1)
    first = lane < DIFF_DH
    hq = proj(C_DQ, W_DQ)
    for h in range(DIFF_HEADS):
        s = rope(hq[:, h * LANES:(h + 1) * LANES]) * (DIFF_DH ** -0.5 * LOG2E)
        kv, g = divmod(h, DIFF_GROUP)
        qh_ref[kv * Q_SLABS + g] = jnp.where(first, s, 0.0).astype(qh_ref.dtype)
        qh_ref[kv * Q_SLABS + DIFF_GROUP + g] = jnp.where(first, 0.0, s).astype(qh_ref.dtype)
    hk = proj(C_DK, W_DK)
    for j in range(DIFF_KV_HEADS):
        s = rope(hk[:, j * LANES:(j + 1) * LANES])
        kf_ref[:, j * LANES:(j + 1) * LANES] = s
        kb_ref[:, j * LANES:(j + 1) * LANES] = s.astype(kb_ref.dtype)
    hv = proj(C_DV, W_DK)
    vf_ref[...] = hv
    vb_ref[...] = hv.astype(vb_ref.dtype)


def _inproj(x2d, gpre, w, wlr, blr, tabs, tm, act_dtype):
    T = x2d.shape[0]
    n_tab = tabs[0].shape[0] // tm
    row = lambda width: pl.BlockSpec((tm, width), lambda i: (i, 0))
    const = lambda shape: pl.BlockSpec(shape, lambda i: (0,) * len(shape))
    tab = pl.BlockSpec((tm, LANES), lambda i: (i % n_tab, 0))
    n_slab = DIFF_KV_HEADS * Q_SLABS
    out_shape = [
        jax.ShapeDtypeStruct((T, W_GLA_Q), act_dtype),
        jax.ShapeDtypeStruct((T, W_GLA_Q), act_dtype),
        jax.ShapeDtypeStruct((T, W_GLA_V), act_dtype),
        jax.ShapeDtypeStruct((T, W_GLA_V), act_dtype),
        jax.ShapeDtypeStruct((T, W_GLA_Q), F32),
        jax.ShapeDtypeStruct((n_slab, T, LANES), act_dtype),
        jax.ShapeDtypeStruct((T, W_DK), F32),
        jax.ShapeDtypeStruct((T, W_DK), act_dtype),
        jax.ShapeDtypeStruct((T, W_DK), F32),
        jax.ShapeDtypeStruct((T, W_DK), act_dtype),
        jax.ShapeDtypeStruct((T, W_DQ), act_dtype),
        jax.ShapeDtypeStruct((T, D_MODEL), act_dtype),
        jax.ShapeDtypeStruct((T, D_MODEL), act_dtype),
    ]
    out_specs = [row(W_GLA_Q), row(W_GLA_Q), row(W_GLA_V), row(W_GLA_V), row(W_GLA_Q),
                 pl.BlockSpec((n_slab, tm, LANES), lambda i: (0, i, 0)),
                 row(W_DK), row(W_DK), row(W_DK), row(W_DK), row(W_DQ), row(D_MODEL), row(D_MODEL)]
    return pl.pallas_call(
        _inproj_kernel,
        out_shape=out_shape,
        grid=(T // tm,),
        in_specs=[row(D_MODEL), const((1, D_MODEL)),
                  pl.BlockSpec((D_MODEL, N_PROJ), lambda i: (0, 0), pipeline_mode=pl.Buffered(1)),
                  const((LANES, W_GLA_Q)), const((1, W_GLA_Q)), tab, tab, tab],
        out_specs=out_specs,
        compiler_params=pltpu.CompilerParams(dimension_semantics=("arbitrary",),
                                             vmem_limit_bytes=VMEM_LIMIT),
        name="inproj",
    )(x2d, gpre, w, wlr, blr, *tabs)


def _gla_kernel(*refs, chunk, n_chunks, has_s0, mxu_dtype):
    if has_s0:
        q_ref, k_ref, v_ref, la_ref, gg_ref, norm_ref, s0_ref, o_ref, sfin_ref, s_scr = refs
    else:
        q_ref, k_ref, v_ref, la_ref, gg_ref, norm_ref, o_ref, sfin_ref, s_scr = refs
    i = pl.program_id(1)

    @pl.when(i == 0)
    def _():
        for h in range(GLA_HEADS):
            if has_s0:
                s_scr[h] = s0_ref[0, h].T
            else:
                s_scr[h] = jnp.zeros(s_scr.shape[1:], F32)

    r = lax.broadcasted_iota(jnp.int32, (chunk, chunk), 0)
    c = lax.broadcasted_iota(jnp.int32, (chunk, chunk), 1)
    causal = r >= c
    norm = norm_ref[...]

    def cumsum_rows(la):
        if chunk < 16:
            return jnp.dot(causal.astype(F32), la, preferred_element_type=F32,
                           precision=lax.Precision.HIGHEST)
        tril = causal.astype(BF16)
        hi = la.astype(BF16)
        lo = (la - hi.astype(F32)).astype(BF16)
        return _dot(tril, hi) + _dot(tril, lo)

    def body(ci, carry):
        sl = pl.ds(pl.multiple_of(ci * chunk, chunk), chunk)
        for h in range(GLA_HEADS):
            kc = slice(h * GLA_DK, (h + 1) * GLA_DK)
            vc = slice(h * GLA_DV, (h + 1) * GLA_DV)
            la = la_ref[sl, kc]
            q = q_ref[sl, kc].astype(F32)
            k = k_ref[sl, kc].astype(F32)
            v = v_ref[sl, vc].astype(mxu_dtype)
            b = cumsum_rows(la)
            bl = b[chunk - 1:chunk, :]
            qt = (q * jnp.exp(b)).astype(mxu_dtype)
            kt = (k * jnp.exp(-b)).astype(mxu_dtype)
            att = jnp.where(causal, _dot_nt(qt, kt), 0.0)
            st_old = s_scr[h]
            o = _dot(att.astype(mxu_dtype), v) + _dot_nt(qt, st_old.astype(mxu_dtype))
            kd = (k * jnp.exp(bl - b)).astype(mxu_dtype)
            s_scr[h] = jnp.exp(bl) * st_old + _dot_tn(v, kd)
            on = o * lax.rsqrt(jnp.mean(o * o, axis=-1, keepdims=True) + EPS) * norm
            o_ref[sl, vc] = (on * gg_ref[sl, vc].astype(F32)).astype(o_ref.dtype)
        return carry

    if n_chunks == 1:
        body(0, 0)
    else:
        lax.fori_loop(0, n_chunks, body, 0, unroll=2)

    @pl.when(i == pl.num_programs(1) - 1)
    def _():
        for h in range(GLA_HEADS):
            sfin_ref[0, h] = s_scr[h].T


def _gla(gq, gk, gv, la, gg, norm, s0, layer, B, L, lb, chunk, act_dtype, mxu_dtype):
    nl = L // lb
    tok = lambda width: pl.BlockSpec((lb, width), lambda b, i: (b * nl + i, 0))
    state_shape = (GLA_HEADS, GLA_DK, GLA_DV)
    in_specs = [tok(W_GLA_Q), tok(W_GLA_Q), tok(W_GLA_V), tok(W_GLA_Q), tok(W_GLA_V),
                pl.BlockSpec((1, GLA_DV), lambda b, i: (0, 0))]
    args = [gq, gk, gv, la, gg, norm]
    if s0 is not None:
        in_specs.append(pl.BlockSpec((None, 1) + state_shape, lambda b, i: (layer, b, 0, 0, 0)))
        args.append(s0)
    return pl.pallas_call(
        functools.partial(_gla_kernel, chunk=chunk, n_chunks=lb // chunk, has_s0=s0 is not None,
                          mxu_dtype=mxu_dtype),
        out_shape=[jax.ShapeDtypeStruct((B * L, W_GLA_V), act_dtype),
                   jax.ShapeDtypeStruct((B,) + state_shape, F32)],
        grid=(B, nl),
        in_specs=in_specs,
        out_specs=[tok(W_GLA_V), pl.BlockSpec((1,) + state_shape, lambda b, i: (b, 0, 0, 0))],
        scratch_shapes=[pltpu.VMEM((GLA_HEADS, GLA_DV, GLA_DK), F32)],
        compiler_params=pltpu.CompilerParams(
            dimension_semantics=("parallel", "arbitrary"), vmem_limit_bytes=VMEM_LIMIT),
        name="gla",
    )(*args)


def _lambda(lam_ref, lam_init):
    lv = lam_ref[...]
    a = jnp.sum(lv[0:1] * lv[1:2], axis=1, keepdims=True)
    b = jnp.sum(lv[2:3] * lv[3:4], axis=1, keepdims=True)
    return jnp.exp(a) - jnp.exp(b) + lam_init


def _diff_finalize(acc, l, lam, norm, lam_init):
    half = acc.shape[0] // 2
    o = acc[:half] / l[:half] - lam * (acc[half:] / l[half:])
    return o * lax.rsqrt(jnp.mean(o * o, axis=-1, keepdims=True) + EPS) * norm * (1.0 - lam_init)


def _attn_prompt_kernel(qh_ref, k_ref, v_ref, dg_ref, lam_ref, norm_ref, o_ref,
                        m_scr, l_scr, acc_scr, s_scr, *, tq, tk, lam_init):
    qi = pl.program_id(2)
    n_lane_tiles = tk // LANES

    m_scr[...] = jnp.full_like(m_scr, -jnp.inf)
    l_scr[...] = jnp.zeros_like(l_scr)
    acc_scr[...] = jnp.zeros_like(acc_scr)

    def tile(j, masked):
        ks = pl.ds(pl.multiple_of(j * tk, tk), tk)
        k = k_ref[ks, :]
        v = v_ref[ks, :]
        if masked:
            r = lax.broadcasted_iota(jnp.int32, (tq, tk), 0)
            c = lax.broadcasted_iota(jnp.int32, (tq, tk), 1)
            visible = j * tk + c <= qi * tq + r
        s_scr[...] = _dot_nt(qh_ref[...].reshape(Q_SLABS * tq, LANES), k)
        for sl in range(Q_SLABS):
            rows = pl.ds(sl * tq, tq)
            s = s_scr[rows, :]
            if masked:
                s = jnp.where(visible, s, -jnp.inf)
            parts = [s[:, t * LANES:(t + 1) * LANES] for t in range(n_lane_tiles)]
            mx = functools.reduce(jnp.maximum, parts)
            m_old = m_scr[rows, :]
            m_new = jnp.maximum(m_old, jnp.max(mx, axis=-1, keepdims=True))
            alpha = jnp.exp2(m_old - m_new)
            ps = [jnp.exp2(part - m_new) for part in parts]
            l_scr[rows, :] = alpha * l_scr[rows, :] + functools.reduce(jnp.add, ps)
            p = jnp.concatenate(ps, axis=1).astype(BF16)
            acc_scr[rows, :] = alpha * acc_scr[rows, :] + _dot(p, v)
            m_scr[rows, :] = m_new

    n_full = (qi * tq) // tk

    def body(j, carry):
        tile(j, False)
        return carry

    lax.fori_loop(0, n_full, body, 0)
    tile(n_full, True)

    lam = _lambda(lam_ref, lam_init)
    l = jnp.sum(l_scr[...], axis=-1, keepdims=True)
    o = _diff_finalize(acc_scr[...], l, lam, norm_ref[...], lam_init)
    for g in range(DIFF_GROUP):
        gate = dg_ref[:, g * LANES:(g + 1) * LANES].astype(F32)
        o_ref[:, g * LANES:(g + 1) * LANES] = (o[g * tq:(g + 1) * tq] * gate).astype(o_ref.dtype)


def _attn_prompt(qh, kb, vb, dg, lam4, norm, B, L, tq, tk, lam_init):
    nq = L // tq
    seq = pl.BlockSpec((L, LANES), lambda b, kv, qi: (b, kv))
    return pl.pallas_call(
        functools.partial(_attn_prompt_kernel, tq=tq, tk=tk, lam_init=lam_init),
        out_shape=jax.ShapeDtypeStruct((B * L, W_DQ), BF16),
        grid=(B, DIFF_KV_HEADS, nq),
        in_specs=[pl.BlockSpec((Q_SLABS, tq, LANES), lambda b, kv, qi: (kv, b * nq + qi, 0)),
                  seq, seq,
                  pl.BlockSpec((tq, DIFF_GROUP * LANES), lambda b, kv, qi: (b * nq + qi, kv)),
                  pl.BlockSpec((4, DIFF_DH), lambda b, kv, qi: (0, 0)),
                  pl.BlockSpec((1, LANES), lambda b, kv, qi: (0, 0))],
        out_specs=pl.BlockSpec((tq, DIFF_GROUP * LANES), lambda b, kv, qi: (b * nq + qi, kv)),
        scratch_shapes=[pltpu.VMEM((Q_SLABS * tq, LANES), F32), pltpu.VMEM((Q_SLABS * tq, LANES), F32),
                        pltpu.VMEM((Q_SLABS * tq, LANES), F32), pltpu.VMEM((Q_SLABS * tq, tk), F32)],
        compiler_params=pltpu.CompilerParams(
            dimension_semantics=("parallel", "parallel", "arbitrary"),
            vmem_limit_bytes=VMEM_LIMIT),
        name="attn_prompt",
    )(qh, kb, vb, dg, lam4, norm)


def _attn_sample_kernel(pt_ref, qh_ref, *refs, pages, t_new, lam_init):
    k_refs = refs[:pages]
    v_refs = refs[pages:2 * pages]
    kn_ref, vn_ref, dg_ref, lam_ref, norm_ref, o_ref, m_scr, l_scr, acc_scr = refs[2 * pages:]
    p = pl.program_id(1)
    nrow = Q_SLABS * t_new
    page = k_refs[0].shape[0] // DIFF_KV_HEADS

    def q_of(kv):
        return qh_ref[kv * Q_SLABS:(kv + 1) * Q_SLABS].reshape(nrow, LANES)

    @pl.when(p == 0)
    def _():
        r = lax.broadcasted_iota(jnp.int32, (Q_SLABS, t_new, LANES), 1).reshape(nrow, LANES)
        c = lax.broadcasted_iota(jnp.int32, (nrow, LANES), 1)
        pad = jnp.zeros((LANES - t_new, LANES), F32)
        for kv in range(DIFF_KV_HEADS):
            rows = pl.ds(kv * nrow, nrow)
            cols = slice(kv * LANES, (kv + 1) * LANES)
            kn = jnp.concatenate([kn_ref[:, cols], pad], axis=0)
            vn = jnp.concatenate([vn_ref[:, cols], pad], axis=0)
            s = jnp.where(c <= r, _dot_nt(q_of(kv), kn), -jnp.inf)
            m = jnp.max(s, axis=-1, keepdims=True)
            e = jnp.exp2(s - m)
            m_scr[rows, :] = m
            l_scr[rows, :] = jnp.sum(e, axis=-1, keepdims=True)
            acc_scr[rows, :] = _dot(e, vn)

    s = jnp.concatenate(
        [jnp.concatenate([_dot_nt(q_of(kv), k_refs[j][pl.ds(kv, page, stride=DIFF_KV_HEADS), :])
                          for j in range(pages)], axis=1)
         for kv in range(DIFF_KV_HEADS)], axis=0)
    m_old = m_scr[...]
    m_new = jnp.maximum(m_old, jnp.max(s, axis=-1, keepdims=True))
    alpha = jnp.exp2(m_old - m_new)
    e = jnp.exp2(s - m_new)
    l_scr[...] = alpha * l_scr[...] + jnp.sum(e, axis=-1, keepdims=True)
    m_scr[...] = m_new
    pv = jnp.concatenate(
        [functools.reduce(jnp.add, [
            _dot(e[kv * nrow:(kv + 1) * nrow, j * page:(j + 1) * page],
                 v_refs[j][pl.ds(kv, page, stride=DIFF_KV_HEADS), :]) for j in range(pages)])
         for kv in range(DIFF_KV_HEADS)], axis=0)
    acc_scr[...] = alpha * acc_scr[...] + pv

    @pl.when(p == pl.num_programs(1) - 1)
    def _():
        lam = _lambda(lam_ref, lam_init)
        for kv in range(DIFF_KV_HEADS):
            rows = pl.ds(kv * nrow, nrow)
            o = _diff_finalize(acc_scr[rows, :], l_scr[rows, :], lam, norm_ref[...], lam_init)
            for g in range(DIFF_GROUP):
                h = kv * DIFF_GROUP + g
                gate = dg_ref[:, h * LANES:(h + 1) * LANES]
                o_ref[:, h * LANES:(h + 1) * LANES] = o[g * t_new:(g + 1) * t_new] * gate


def _attn_sample(page_table, qh, cache_k, cache_v, layer, kn, vn, dg, lam4, norm, t_new, pages, lam_init):
    nb, n_pages = page_table.shape
    page_rows = cache_k.shape[2]
    pt = page_table.reshape(-1)

    def page_spec(j):
        return pl.BlockSpec((None, None, page_rows, LANES),
                            lambda b, p, pt_ref: (layer, pt_ref[b * n_pages + p * pages + j], 0, 0))

    tokrow = lambda width: pl.BlockSpec((t_new, width), lambda b, p, pt_ref: (b, 0))
    nrow = Q_SLABS * t_new * DIFF_KV_HEADS
    grid_spec = pltpu.PrefetchScalarGridSpec(
        num_scalar_prefetch=1,
        grid=(nb, n_pages // pages),
        in_specs=[pl.BlockSpec((DIFF_KV_HEADS * Q_SLABS, t_new, LANES), lambda b, p, pt_ref: (0, b, 0))]
                 + [page_spec(j) for j in range(pages)] * 2
                 + [tokrow(W_DK), tokrow(W_DK), tokrow(W_DQ),
                    pl.BlockSpec((4, DIFF_DH), lambda b, p, pt_ref: (0, 0)),
                    pl.BlockSpec((1, LANES), lambda b, p, pt_ref: (0, 0))],
        out_specs=tokrow(W_DQ),
        scratch_shapes=[pltpu.VMEM((nrow, 1), F32), pltpu.VMEM((nrow, 1), F32),
                        pltpu.VMEM((nrow, LANES), F32)],
    )
    return pl.pallas_call(
        functools.partial(_attn_sample_kernel, pages=pages, t_new=t_new, lam_init=lam_init),
        out_shape=jax.ShapeDtypeStruct((nb * t_new, W_DQ), F32),
        grid_spec=grid_spec,
        compiler_params=pltpu.CompilerParams(dimension_semantics=("parallel", "arbitrary"),
                                             vmem_limit_bytes=VMEM_LIMIT),
        name="attn_sample",
    )(pt, qh, *([cache_k] * pages), *([cache_v] * pages), kn, vn, dg, lam4, norm)


def _merge_kernel(x_ref, oa_ref, ob_ref, ga_ref, gb_ref, wpa_ref, wpb_ref, wo_ref, gpost_ref, y_ref):
    pa = _dot(oa_ref[...].astype(BF16), wpa_ref[...])
    pb = _dot(ob_ref[...].astype(BF16), wpb_ref[...])
    merged = ga_ref[...].astype(F32) * pa + gb_ref[...].astype(F32) * pb
    y = _dot(merged.astype(BF16), wo_ref[...])
    yn = y * lax.rsqrt(jnp.mean(y * y, axis=-1, keepdims=True) + EPS) * gpost_ref[...]
    y_ref[...] = x_ref[...] + yn


def _merge(x2d, oa, ob, ga, gb, wpa, wpb, wo, gpost, tm):
    T = x2d.shape[0]
    row = pl.BlockSpec((tm, D_MODEL), lambda i: (i, 0))
    wspec = pl.BlockSpec((D_MODEL, D_MODEL), lambda i: (0, 0))
    return pl.pallas_call(
        _merge_kernel,
        out_shape=jax.ShapeDtypeStruct((T, D_MODEL), F32),
        grid=(T // tm,),
        in_specs=[row, row, row, row, row, wspec, wspec, wspec,
                  pl.BlockSpec((1, D_MODEL), lambda i: (0, 0))],
        out_specs=row,
        compiler_params=pltpu.CompilerParams(dimension_semantics=("parallel",),
                                             vmem_limit_bytes=VMEM_LIMIT),
        name="merge",
    )(x2d, oa, ob, ga, gb, wpa, wpb, wo, gpost)


def _rope_tables(pos):
    half = ROT_DIM // 2
    lane = np.arange(LANES) % DIFF_DH
    inv = ROPE_THETA ** (-(jnp.arange(half, dtype=F32) * 2.0) / ROT_DIM)
    ang = pos.astype(F32)[:, None] * inv[None, :]
    cos_l = jnp.cos(ang)[:, lane % half]
    sin_l = jnp.sin(ang)[:, lane % half]
    rot = jnp.asarray(lane < ROT_DIM)[None, :]
    upper = jnp.asarray((lane >= half) & (lane < ROT_DIM))[None, :]
    lower = jnp.asarray(lane < half)[None, :]
    return (jnp.where(rot, cos_l, 1.0), jnp.where(upper, sin_l, 0.0), jnp.where(lower, -sin_l, 0.0))


def _prep_weights(w_in_l, gla_w_lr_l):
    c_lr = 2 * W_GLA_Q + 2 * W_GLA_V
    w = jnp.concatenate([w_in_l[:, :c_lr], w_in_l[:, c_lr + GLA_RANK:], w_in_l[:, c_lr:c_lr + GLA_RANK],
                         jnp.zeros((D_MODEL, LANES - GLA_RANK), F32)], axis=1).astype(BF16)
    wlr = jnp.concatenate([gla_w_lr_l, jnp.zeros((LANES - GLA_RANK, W_GLA_Q), F32)], axis=0).astype(BF16)
    return w, wlr


def kernel(x_prompt, x_sample, cache_k, cache_v, state_gla, page_table, w_in, gla_w_lr, gla_b_lr,
           gla_norm, lam_q1, lam_k1, lam_q2, lam_k2, diff_norm, w_pa, w_pb, w_o, g_pre, g_post):
    depth = w_in.shape[0]
    bp, lp, _ = x_prompt.shape
    bs, ls, _ = x_sample.shape
    n_pool, page = cache_k.shape[1], cache_k.shape[2]
    past = page_table.shape[1] * page
    ck = cache_k.reshape(depth, n_pool, page * DIFF_KV_HEADS, LANES)
    cv = cache_v.reshape(depth, n_pool, page * DIFF_KV_HEADS, LANES)

    tabs_p = _rope_tables(jnp.arange(lp, dtype=jnp.int32))
    tabs_s = _rope_tables(jnp.tile(past + jnp.arange(ls, dtype=jnp.int32), bs))

    xp = x_prompt.reshape(bp * lp, D_MODEL)
    xs = x_sample.reshape(bs * ls, D_MODEL)
    outs = [[] for _ in range(6)]
    for l in range(depth):
        lam_init = 0.8 - 0.6 * math.exp(-0.3 * l)
        w, wlr = _prep_weights(w_in[l], gla_w_lr[l])
        blr = gla_b_lr[l][None, :]
        gpre = g_pre[l][None, :]
        gpost = g_post[l][None, :]
        gnorm = gla_norm[l][None, :]
        dnorm = diff_norm[l][None, :]
        lam4 = jnp.stack([lam_q1[l], lam_k1[l], lam_q2[l], lam_k2[l]])
        wpa, wpb, wo = w_pa[l].astype(BF16), w_pb[l].astype(BF16), w_o[l].astype(BF16)

        gq, gk, gv, gg, la, qh, kf, kb, vf, vb, dg, ga, gb = _inproj(
            xp, gpre, w, wlr, blr, tabs_p, INPROJ_TM, BF16)
        oa, sfin_p = _gla(gq, gk, gv, la, gg, gnorm, None, l, bp, lp, GLA_BLOCK, math.gcd(lp, GLA_CHUNK),
                          BF16, BF16)
        ob = _attn_prompt(qh, kb, vb, dg, lam4, dnorm, bp, lp, ATTN_TQ, ATTN_TK, lam_init)
        xp = _merge(xp, oa, ob, ga, gb, wpa, wpb, wo, gpost, MERGE_TM)
        outs[0].append(kf.reshape(bp, lp, DIFF_KV_HEADS, 2 * DIFF_DH))
        outs[1].append(vf.reshape(bp, lp, DIFF_KV_HEADS, 2 * DIFF_DH))
        outs[2].append(sfin_p)

        gq, gk, gv, gg, la, qh, kf, _, vf, _, dg, ga, gb = _inproj(
            xs, gpre, w, wlr, blr, tabs_s, bs * ls, F32)
        oa, sfin_s = _gla(gq, gk, gv, la, gg, gnorm, state_gla, l, bs, ls, ls, math.gcd(ls, GLA_CHUNK),
                          F32, F32)
        ob = _attn_sample(page_table, qh, ck, cv, l, kf, vf, dg, lam4, dnorm, ls, SAMPLE_PAGES_PER_STEP,
                          lam_init)
        xs = _merge(xs, oa, ob, ga, gb, wpa, wpb, wo, gpost, bs * ls)
        outs[3].append(kf.reshape(bs, ls, DIFF_KV_HEADS, 2 * DIFF_DH))
        outs[4].append(vf.reshape(bs, ls, DIFF_KV_HEADS, 2 * DIFF_DH))
        outs[5].append(sfin_s)

    return (xp.reshape(bp, lp, D_MODEL), xs.reshape(bs, ls, D_MODEL),
            jnp.stack(outs[0]), jnp.stack(outs[1]), jnp.stack(outs[2]),
            jnp.stack(outs[3]), jnp.stack(outs[4]), jnp.stack(outs[5]))
```

```python
import functools
import math

import jax
import jax.numpy as jnp
import numpy as np
from jax import lax
from jax.experimental import pallas as pl
from jax.experimental.pallas import tpu as pltpu

F32 = jnp.float32
BF16 = jnp.bfloat16

D_MODEL = 1024
GLA_HEADS = 4
GLA_DK = 128
GLA_DV = 256
GLA_RANK = 16
GLA_TAU = 16.0
GLA_CHUNK = 64
DIFF_HEADS = 8
DIFF_KV_HEADS = 4
DIFF_GROUP = 2
DIFF_DH = 64
ROT_DIM = 16
ROPE_THETA = 500000.0
EPS = 1e-6
LOG2E = math.log2(math.e)
LANES = 128
VMEM_LIMIT = 56 * 1024 * 1024

W_GLA_Q = GLA_HEADS * GLA_DK
W_GLA_V = GLA_HEADS * GLA_DV
W_DQ = DIFF_HEADS * 2 * DIFF_DH
W_DK = DIFF_KV_HEADS * 2 * DIFF_DH
Q_SLABS = 2 * DIFF_GROUP

W_GROUP_A = 2 * W_GLA_Q + 2 * W_GLA_V
W_GROUP_B = W_DQ + 2 * W_DK + W_DQ + 2 * D_MODEL
C_GQ = 0
C_GK = C_GQ + W_GLA_Q
C_GV = C_GK + W_GLA_Q
C_GG = C_GV + W_GLA_V
C_DQ = 0
C_DK = C_DQ + W_DQ
C_DV = C_DK + W_DK
C_DG = C_DV + W_DK
C_GA = C_DG + W_DQ
C_GB = C_GA + D_MODEL

INPROJ_TM = 256
GLA_BLOCK = 512
ATTN_TQ = 512
ATTN_TK = 512
MERGE_TM = 512
SAMPLE_PAGES_PER_STEP = 16


def _silu(x):
    return x * jax.nn.sigmoid(x)


def _dot(a, b):
    return jnp.dot(a, b, preferred_element_type=F32)


def _dot_nt(a, b):
    return lax.dot_general(a, b, (((1,), (1,)), ((), ())), preferred_element_type=F32)


def _dot_tn(a, b, precision=None):
    return lax.dot_general(a, b, (((0,), (0,)), ((), ())), preferred_element_type=F32,
                           precision=precision)


def _inproj_kernel(x_ref, gpre_ref, wa_ref, wb_ref, wr_ref, wlr_ref, blr_ref, cos_ref, sup_ref, sdn_ref,
                   gq_ref, gk_ref, gv_ref, gg_ref, la_ref, qh_ref, kf_ref, kb_ref,
                   vf_ref, vb_ref, dg_ref, ga_ref, gb_ref):
    x = x_ref[...]
    ms = jnp.mean(x * x, axis=-1, keepdims=True)
    xn = (x * lax.rsqrt(ms + EPS) * gpre_ref[...]).astype(BF16)

    def proj(w_ref, c0, width):
        return _dot(xn, w_ref[:, c0:c0 + width])

    gq_ref[...] = (proj(wa_ref, C_GQ, W_GLA_Q) * (GLA_DK ** -0.5)).astype(gq_ref.dtype)
    gk_ref[...] = proj(wa_ref, C_GK, W_GLA_Q).astype(gk_ref.dtype)
    gv_ref[...] = proj(wa_ref, C_GV, W_GLA_V).astype(gv_ref.dtype)
    gg_ref[...] = _silu(proj(wa_ref, C_GG, W_GLA_V)).astype(gg_ref.dtype)
    dg_ref[...] = _silu(proj(wb_ref, C_DG, W_DQ)).astype(dg_ref.dtype)
    ga_ref[...] = jax.nn.sigmoid(proj(wb_ref, C_GA, D_MODEL)).astype(ga_ref.dtype)
    gb_ref[...] = jax.nn.sigmoid(proj(wb_ref, C_GB, D_MODEL)).astype(gb_ref.dtype)

    glr = _dot(xn, wr_ref[...]).astype(BF16)
    z = _dot(glr, wlr_ref[...]) + blr_ref[...]
    la_ref[...] = (jnp.minimum(z, 0.0) - jnp.log(1.0 + jnp.exp(-jnp.abs(z)))) * (1.0 / GLA_TAU)

    cos = cos_ref[...]
    sup = sup_ref[...]
    sdn = sdn_ref[...]

    def rope(s):
        return s * cos + pltpu.roll(s, 8, 1) * sup + pltpu.roll(s, LANES - 8, 1) * sdn

    lane = lax.broadcasted_iota(jnp.int32, (1, LANES), 1)
    first = lane < DIFF_DH
    hq = proj(wb_ref, C_DQ, W_DQ)
    for h in range(DIFF_HEADS):
        s = rope(hq[:, h * LANES:(h + 1) * LANES]) * (DIFF_DH ** -0.5 * LOG2E)
        kv, g = divmod(h, DIFF_GROUP)
        qh_ref[kv * Q_SLABS + g] = jnp.where(first, s, 0.0).astype(qh_ref.dtype)
        qh_ref[kv * Q_SLABS + DIFF_GROUP + g] = jnp.where(first, 0.0, s).astype(qh_ref.dtype)
    tm = x.shape[0]
    hk = proj(wb_ref, C_DK, W_DK)
    hv = proj(wb_ref, C_DV, W_DK)
    for j in range(DIFF_KV_HEADS):
        head_rows = pl.ds(j, tm, stride=DIFF_KV_HEADS)
        s = rope(hk[:, j * LANES:(j + 1) * LANES])
        kf_ref[head_rows, :] = s
        kb_ref[:, j * LANES:(j + 1) * LANES] = s.astype(kb_ref.dtype)
        vf_ref[head_rows, :] = hv[:, j * LANES:(j + 1) * LANES]
    vb_ref[...] = hv.astype(vb_ref.dtype)


def _inproj(x2d, gpre, wa, wb, wr, wlr, blr, tabs, tm, act_dtype):
    T = x2d.shape[0]
    n_tab = tabs[0].shape[0] // tm
    row = lambda width: pl.BlockSpec((tm, width), lambda i: (i, 0))
    const = lambda shape: pl.BlockSpec(shape, lambda i: (0,) * len(shape))
    tab = pl.BlockSpec((tm, LANES), lambda i: (i % n_tab, 0))
    cache_rows = pl.BlockSpec((tm * DIFF_KV_HEADS, LANES), lambda i: (i, 0))
    resident = lambda width: pl.BlockSpec((D_MODEL, width), lambda i: (0, 0), pipeline_mode=pl.Buffered(1))
    n_slab = DIFF_KV_HEADS * Q_SLABS
    out_shape = [
        jax.ShapeDtypeStruct((T, W_GLA_Q), act_dtype),
        jax.ShapeDtypeStruct((T, W_GLA_Q), act_dtype),
        jax.ShapeDtypeStruct((T, W_GLA_V), act_dtype),
        jax.ShapeDtypeStruct((T, W_GLA_V), act_dtype),
        jax.ShapeDtypeStruct((T, W_GLA_Q), F32),
        jax.ShapeDtypeStruct((n_slab, T, LANES), act_dtype),
        jax.ShapeDtypeStruct((T * DIFF_KV_HEADS, LANES), F32),
        jax.ShapeDtypeStruct((T, W_DK), act_dtype),
        jax.ShapeDtypeStruct((T * DIFF_KV_HEADS, LANES), F32),
        jax.ShapeDtypeStruct((T, W_DK), act_dtype),
        jax.ShapeDtypeStruct((T, W_DQ), act_dtype),
        jax.ShapeDtypeStruct((T, D_MODEL), act_dtype),
        jax.ShapeDtypeStruct((T, D_MODEL), act_dtype),
    ]
    out_specs = [row(W_GLA_Q), row(W_GLA_Q), row(W_GLA_V), row(W_GLA_V), row(W_GLA_Q),
                 pl.BlockSpec((n_slab, tm, LANES), lambda i: (0, i, 0)),
                 cache_rows, row(W_DK), cache_rows, row(W_DK), row(W_DQ), row(D_MODEL), row(D_MODEL)]
    return pl.pallas_call(
        _inproj_kernel,
        out_shape=out_shape,
        grid=(T // tm,),
        in_specs=[row(D_MODEL), const((1, D_MODEL)),
                  resident(W_GROUP_A), resident(W_GROUP_B), resident(LANES),
                  const((LANES, W_GLA_Q)), const((1, W_GLA_Q)), tab, tab, tab],
        out_specs=out_specs,
        compiler_params=pltpu.CompilerParams(dimension_semantics=("arbitrary",),
                                             vmem_limit_bytes=VMEM_LIMIT),
        name="inproj",
    )(x2d, gpre, wa, wb, wr, wlr, blr, *tabs)


def _gla_kernel(*refs, chunk, n_chunks, has_s0, mxu_dtype):
    if has_s0:
        q_ref, k_ref, v_ref, la_ref, gg_ref, norm_ref, s0_ref, o_ref, sfin_ref, s_scr = refs
    else:
        q_ref, k_ref, v_ref, la_ref, gg_ref, norm_ref, o_ref, sfin_ref, s_scr = refs
    i = pl.program_id(1)

    @pl.when(i == 0)
    def _():
        for h in range(GLA_HEADS):
            if has_s0:
                s_scr[h] = s0_ref[0, h].T
            else:
                s_scr[h] = jnp.zeros(s_scr.shape[1:], F32)

    r = lax.broadcasted_iota(jnp.int32, (chunk, chunk), 0)
    c = lax.broadcasted_iota(jnp.int32, (chunk, chunk), 1)
    causal = r >= c
    norm = norm_ref[...]

    def cumsum_rows(la):
        if chunk < 16:
            return jnp.dot(causal.astype(F32), la, preferred_element_type=F32,
                           precision=lax.Precision.HIGHEST)
        tril = causal.astype(BF16)
        hi = la.astype(BF16)
        lo = (la - hi.astype(F32)).astype(BF16)
        return _dot(tril, hi) + _dot(tril, lo)

    def body(ci, carry):
        sl = pl.ds(pl.multiple_of(ci * chunk, chunk), chunk)
        for h in range(GLA_HEADS):
            kc = slice(h * GLA_DK, (h + 1) * GLA_DK)
            vc = slice(h * GLA_DV, (h + 1) * GLA_DV)
            la = la_ref[sl, kc]
            q = q_ref[sl, kc].astype(F32)
            k = k_ref[sl, kc].astype(F32)
            v = v_ref[sl, vc].astype(mxu_dtype)
            b = cumsum_rows(la)
            bl = b[chunk - 1:chunk, :]
            qt = (q * jnp.exp(b)).astype(mxu_dtype)
            kt = (k * jnp.exp(-b)).astype(mxu_dtype)
            att = jnp.where(causal, _dot_nt(qt, kt), 0.0)
            st_old = s_scr[h]
            o = _dot(att.astype(mxu_dtype), v) + _dot_nt(qt, st_old.astype(mxu_dtype))
            kd = (k * jnp.exp(bl - b)).astype(mxu_dtype)
            s_scr[h] = jnp.exp(bl) * st_old + _dot_tn(v, kd)
            on = o * lax.rsqrt(jnp.mean(o * o, axis=-1, keepdims=True) + EPS) * norm
            o_ref[sl, vc] = (on * gg_ref[sl, vc].astype(F32)).astype(o_ref.dtype)
        return carry

    if n_chunks == 1:
        body(0, 0)
    else:
        lax.fori_loop(0, n_chunks, body, 0, unroll=4)

    @pl.when(i == pl.num_programs(1) - 1)
    def _():
        for h in range(GLA_HEADS):
            sfin_ref[0, h] = s_scr[h].T


def _gla(gq, gk, gv, la, gg, norm, s0, layer, B, L, lb, chunk, act_dtype, mxu_dtype):
    nl = L // lb
    tok = lambda width: pl.BlockSpec((lb, width), lambda b, i: (b * nl + i, 0))
    state_shape = (GLA_HEADS, GLA_DK, GLA_DV)
    in_specs = [tok(W_GLA_Q), tok(W_GLA_Q), tok(W_GLA_V), tok(W_GLA_Q), tok(W_GLA_V),
                pl.BlockSpec((1, GLA_DV), lambda b, i: (0, 0))]
    args = [gq, gk, gv, la, gg, norm]
    if s0 is not None:
        in_specs.append(pl.BlockSpec((None, 1) + state_shape, lambda b, i: (layer, b, 0, 0, 0)))
        args.append(s0)
    return pl.pallas_call(
        functools.partial(_gla_kernel, chunk=chunk, n_chunks=lb // chunk, has_s0=s0 is not None,
                          mxu_dtype=mxu_dtype),
        out_shape=[jax.ShapeDtypeStruct((B * L, W_GLA_V), act_dtype),
                   jax.ShapeDtypeStruct((B,) + state_shape, F32)],
        grid=(B, nl),
        in_specs=in_specs,
        out_specs=[tok(W_GLA_V), pl.BlockSpec((1,) + state_shape, lambda b, i: (b, 0, 0, 0))],
        scratch_shapes=[pltpu.VMEM((GLA_HEADS, GLA_DV, GLA_DK), F32)],
        compiler_params=pltpu.CompilerParams(
            dimension_semantics=("parallel", "arbitrary"), vmem_limit_bytes=VMEM_LIMIT),
        name="gla",
    )(*args)


def _lambda(lam_ref, lam_init):
    lv = lam_ref[...]
    a = jnp.sum(lv[0:1] * lv[1:2], axis=1, keepdims=True)
    b = jnp.sum(lv[2:3] * lv[3:4], axis=1, keepdims=True)
    return jnp.exp(a) - jnp.exp(b) + lam_init


def _diff_finalize(acc, l, lam, norm, lam_init):
    half = acc.shape[0] // 2
    o = acc[:half] / l[:half] - lam * (acc[half:] / l[half:])
    return o * lax.rsqrt(jnp.mean(o * o, axis=-1, keepdims=True) + EPS) * norm * (1.0 - lam_init)


def _attn_prompt_kernel(qh_ref, k_ref, v_ref, dg_ref, lam_ref, norm_ref, o_ref,
                        m_scr, l_scr, acc_scr, s_scr, *, tq, tk, lam_init):
    qi = pl.program_id(2)
    n_lane_tiles = tk // LANES

    m_scr[...] = jnp.full_like(m_scr, -jnp.inf)
    l_scr[...] = jnp.zeros_like(l_scr)
    acc_scr[...] = jnp.zeros_like(acc_scr)

    def tile(j, masked, slot):
        s_buf = s_scr.at[slot]
        ks = pl.ds(pl.multiple_of(j * tk, tk), tk)
        k = k_ref[ks, :]
        v = v_ref[ks, :]
        if masked:
            r = lax.broadcasted_iota(jnp.int32, (tq, tk), 0)
            c = lax.broadcasted_iota(jnp.int32, (tq, tk), 1)
            visible = j * tk + c <= qi * tq + r
        s_buf[...] = _dot_nt(qh_ref[...].reshape(Q_SLABS * tq, LANES), k)
        for sl in range(Q_SLABS):
            rows = pl.ds(sl * tq, tq)
            s = s_buf[rows, :]
            if masked:
                s = jnp.where(visible, s, -jnp.inf)
            parts = [s[:, t * LANES:(t + 1) * LANES] for t in range(n_lane_tiles)]
            mx = functools.reduce(jnp.maximum, parts)
            m_old = m_scr[rows, :]
            m_new = jnp.maximum(m_old, jnp.max(mx, axis=-1, keepdims=True))
            alpha = jnp.exp2(m_old - m_new)
            ps = [jnp.exp2(part - m_new) for part in parts]
            l_scr[rows, :] = alpha * l_scr[rows, :] + functools.reduce(jnp.add, ps)
            p = jnp.concatenate(ps, axis=1).astype(BF16)
            acc_scr[rows, :] = alpha * acc_scr[rows, :] + _dot(p, v)
            m_scr[rows, :] = m_new

    n_full = (qi * tq) // tk

    def pair(jj, carry):
        tile(2 * jj, False, 0)
        tile(2 * jj + 1, False, 1)
        return carry

    lax.fori_loop(0, n_full // 2, pair, 0)

    @pl.when(n_full % 2 == 1)
    def _():
        tile(n_full - 1, False, 0)

    tile(n_full, True, 1)

    lam = _lambda(lam_ref, lam_init)
    l = jnp.sum(l_scr[...], axis=-1, keepdims=True)
    o = _diff_finalize(acc_scr[...], l, lam, norm_ref[...], lam_init)
    for g in range(DIFF_GROUP):
        gate = dg_ref[:, g * LANES:(g + 1) * LANES].astype(F32)
        o_ref[:, g * LANES:(g + 1) * LANES] = (o[g * tq:(g + 1) * tq] * gate).astype(o_ref.dtype)


def _attn_prompt(qh, kb, vb, dg, lam4, norm, B, L, tq, tk, lam_init):
    nq = L // tq
    seq = pl.BlockSpec((L, LANES), lambda b, kv, qi: (b, kv))
    return pl.pallas_call(
        functools.partial(_attn_prompt_kernel, tq=tq, tk=tk, lam_init=lam_init),
        out_shape=jax.ShapeDtypeStruct((B * L, W_DQ), BF16),
        grid=(B, DIFF_KV_HEADS, nq),
        in_specs=[pl.BlockSpec((Q_SLABS, tq, LANES), lambda b, kv, qi: (kv, b * nq + qi, 0)),
                  seq, seq,
                  pl.BlockSpec((tq, DIFF_GROUP * LANES), lambda b, kv, qi: (b * nq + qi, kv)),
                  pl.BlockSpec((4, DIFF_DH), lambda b, kv, qi: (0, 0)),
                  pl.BlockSpec((1, LANES), lambda b, kv, qi: (0, 0))],
        out_specs=pl.BlockSpec((tq, DIFF_GROUP * LANES), lambda b, kv, qi: (b * nq + qi, kv)),
        scratch_shapes=[pltpu.VMEM((Q_SLABS * tq, LANES), F32), pltpu.VMEM((Q_SLABS * tq, LANES), F32),
                        pltpu.VMEM((Q_SLABS * tq, LANES), F32), pltpu.VMEM((2, Q_SLABS * tq, tk), F32)],
        compiler_params=pltpu.CompilerParams(
            dimension_semantics=("parallel", "parallel", "arbitrary"),
            vmem_limit_bytes=VMEM_LIMIT),
        name="attn_prompt",
    )(qh, kb, vb, dg, lam4, norm)


def _attn_sample_kernel(pt_ref, qh_ref, *refs, pages, t_new, lam_init):
    k_refs = refs[:pages]
    v_refs = refs[pages:2 * pages]
    kn_ref, vn_ref, dg_ref, lam_ref, norm_ref, o_ref, m_scr, l_scr, acc_scr = refs[2 * pages:]
    p = pl.program_id(1)
    nrow = Q_SLABS * t_new
    page = k_refs[0].shape[0] // DIFF_KV_HEADS

    def q_of(kv):
        return qh_ref[kv * Q_SLABS:(kv + 1) * Q_SLABS].reshape(nrow, LANES)

    @pl.when(p == 0)
    def _():
        r = lax.broadcasted_iota(jnp.int32, (Q_SLABS, t_new, LANES), 1).reshape(nrow, LANES)
        c = lax.broadcasted_iota(jnp.int32, (nrow, LANES), 1)
        pad = jnp.zeros((LANES - t_new, LANES), F32)
        for kv in range(DIFF_KV_HEADS):
            rows = pl.ds(kv * nrow, nrow)
            cols = slice(kv * LANES, (kv + 1) * LANES)
            head_rows = pl.ds(kv, t_new, stride=DIFF_KV_HEADS)
            kn = jnp.concatenate([kn_ref[head_rows, :], pad], axis=0)
            vn = jnp.concatenate([vn_ref[head_rows, :], pad], axis=0)
            s = jnp.where(c <= r, _dot_nt(q_of(kv), kn), -jnp.inf)
            m = jnp.max(s, axis=-1, keepdims=True)
            e = jnp.exp2(s - m)
            m_scr[rows, :] = m
            l_scr[rows, :] = jnp.sum(e, axis=-1, keepdims=True)
            acc_scr[rows, :] = _dot(e, vn)

    s = jnp.concatenate(
        [jnp.concatenate([_dot_nt(q_of(kv), k_refs[j][pl.ds(kv, page, stride=DIFF_KV_HEADS), :])
                          for j in range(pages)], axis=1)
         for kv in range(DIFF_KV_HEADS)], axis=0)
    m_old = m_scr[...]
    m_new = jnp.maximum(m_old, jnp.max(s, axis=-1, keepdims=True))
    alpha = jnp.exp2(m_old - m_new)
    e = jnp.exp2(s - m_new)
    l_scr[...] = alpha * l_scr[...] + jnp.sum(e, axis=-1, keepdims=True)
    m_scr[...] = m_new
    pv = jnp.concatenate(
        [functools.reduce(jnp.add, [
            _dot(e[kv * nrow:(kv + 1) * nrow, j * page:(j + 1) * page],
                 v_refs[j][pl.ds(kv, page, stride=DIFF_KV_HEADS), :]) for j in range(pages)])
         for kv in range(DIFF_KV_HEADS)], axis=0)
    acc_scr[...] = alpha * acc_scr[...] + pv

    @pl.when(p == pl.num_programs(1) - 1)
    def _():
        lam = _lambda(lam_ref, lam_init)
        for kv in range(DIFF_KV_HEADS):
            rows = pl.ds(kv * nrow, nrow)
            o = _diff_finalize(acc_scr[rows, :], l_scr[rows, :], lam, norm_ref[...], lam_init)
            for g in range(DIFF_GROUP):
                h = kv * DIFF_GROUP + g
                gate = dg_ref[:, h * LANES:(h + 1) * LANES]
                o_ref[:, h * LANES:(h + 1) * LANES] = o[g * t_new:(g + 1) * t_new] * gate


def _attn_sample(page_table, qh, cache_k, cache_v, layer, kn, vn, dg, lam4, norm, t_new, pages, lam_init):
    nb, n_pages = page_table.shape
    page_rows = cache_k.shape[2]
    pt = page_table.reshape(-1)

    def page_spec(j):
        return pl.BlockSpec((None, None, page_rows, LANES),
                            lambda b, p, pt_ref: (layer, pt_ref[b * n_pages + p * pages + j], 0, 0))

    tokrow = lambda width: pl.BlockSpec((t_new, width), lambda b, p, pt_ref: (b, 0))
    nrow = Q_SLABS * t_new * DIFF_KV_HEADS
    grid_spec = pltpu.PrefetchScalarGridSpec(
        num_scalar_prefetch=1,
        grid=(nb, n_pages // pages),
        in_specs=[pl.BlockSpec((DIFF_KV_HEADS * Q_SLABS, t_new, LANES), lambda b, p, pt_ref: (0, b, 0))]
                 + [page_spec(j) for j in range(pages)] * 2
                 + [pl.BlockSpec((t_new * DIFF_KV_HEADS, LANES), lambda b, p, pt_ref: (b, 0))] * 2
                 + [tokrow(W_DQ),
                    pl.BlockSpec((4, DIFF_DH), lambda b, p, pt_ref: (0, 0)),
                    pl.BlockSpec((1, LANES), lambda b, p, pt_ref: (0, 0))],
        out_specs=tokrow(W_DQ),
        scratch_shapes=[pltpu.VMEM((nrow, 1), F32), pltpu.VMEM((nrow, 1), F32),
                        pltpu.VMEM((nrow, LANES), F32)],
    )
    return pl.pallas_call(
        functools.partial(_attn_sample_kernel, pages=pages, t_new=t_new, lam_init=lam_init),
        out_shape=jax.ShapeDtypeStruct((nb * t_new, W_DQ), F32),
        grid_spec=grid_spec,
        compiler_params=pltpu.CompilerParams(dimension_semantics=("parallel", "arbitrary"),
                                             vmem_limit_bytes=VMEM_LIMIT),
        name="attn_sample",
    )(pt, qh, *([cache_k] * pages), *([cache_v] * pages), kn, vn, dg, lam4, norm)


def _merge_kernel(x_ref, oa_ref, ob_ref, ga_ref, gb_ref, wpa_ref, wpb_ref, wo_ref, gpost_ref, y_ref):
    pa = _dot(oa_ref[...].astype(BF16), wpa_ref[...])
    pb = _dot(ob_ref[...].astype(BF16), wpb_ref[...])
    merged = ga_ref[...].astype(F32) * pa + gb_ref[...].astype(F32) * pb
    y = _dot(merged.astype(BF16), wo_ref[...])
    yn = y * lax.rsqrt(jnp.mean(y * y, axis=-1, keepdims=True) + EPS) * gpost_ref[...]
    y_ref[...] = x_ref[...] + yn


def _merge(x2d, oa, ob, ga, gb, wpa, wpb, wo, gpost, tm):
    T = x2d.shape[0]
    row = pl.BlockSpec((tm, D_MODEL), lambda i: (i, 0))
    wspec = pl.BlockSpec((D_MODEL, D_MODEL), lambda i: (0, 0))
    return pl.pallas_call(
        _merge_kernel,
        out_shape=jax.ShapeDtypeStruct((T, D_MODEL), F32),
        grid=(T // tm,),
        in_specs=[row, row, row, row, row, wspec, wspec, wspec,
                  pl.BlockSpec((1, D_MODEL), lambda i: (0, 0))],
        out_specs=row,
        compiler_params=pltpu.CompilerParams(dimension_semantics=("parallel",),
                                             vmem_limit_bytes=VMEM_LIMIT),
        name="merge",
    )(x2d, oa, ob, ga, gb, wpa, wpb, wo, gpost)


def _rope_tables(pos):
    half = ROT_DIM // 2
    lane = np.arange(LANES) % DIFF_DH
    inv = ROPE_THETA ** (-(jnp.arange(half, dtype=F32) * 2.0) / ROT_DIM)
    ang = pos.astype(F32)[:, None] * inv[None, :]
    cos_l = jnp.cos(ang)[:, lane % half]
    sin_l = jnp.sin(ang)[:, lane % half]
    rot = jnp.asarray(lane < ROT_DIM)[None, :]
    upper = jnp.asarray((lane >= half) & (lane < ROT_DIM))[None, :]
    lower = jnp.asarray(lane < half)[None, :]
    return (jnp.where(rot, cos_l, 1.0), jnp.where(upper, sin_l, 0.0), jnp.where(lower, -sin_l, 0.0))


def _prep_weights(w_in, gla_w_lr, layer):
    wa = w_in[layer, :, :W_GROUP_A].astype(BF16)
    wb = w_in[layer, :, W_GROUP_A + GLA_RANK:].astype(BF16)
    wr = jnp.pad(w_in[layer, :, W_GROUP_A:W_GROUP_A + GLA_RANK], ((0, 0), (0, LANES - GLA_RANK))).astype(BF16)
    wlr = jnp.pad(gla_w_lr[layer], ((0, LANES - GLA_RANK), (0, 0))).astype(BF16)
    return wa, wb, wr, wlr


def kernel(x_prompt, x_sample, cache_k, cache_v, state_gla, page_table, w_in, gla_w_lr, gla_b_lr,
           gla_norm, lam_q1, lam_k1, lam_q2, lam_k2, diff_norm, w_pa, w_pb, w_o, g_pre, g_post):
    depth = w_in.shape[0]
    bp, lp, _ = x_prompt.shape
    bs, ls, _ = x_sample.shape
    n_pool, page = cache_k.shape[1], cache_k.shape[2]
    past = page_table.shape[1] * page
    ck = cache_k.reshape(depth, n_pool, page * DIFF_KV_HEADS, LANES)
    cv = cache_v.reshape(depth, n_pool, page * DIFF_KV_HEADS, LANES)

    tabs_p = _rope_tables(jnp.arange(lp, dtype=jnp.int32))
    tabs_s = _rope_tables(jnp.tile(past + jnp.arange(ls, dtype=jnp.int32), bs))

    xp = x_prompt.reshape(bp * lp, D_MODEL)
    xs = x_sample.reshape(bs * ls, D_MODEL)
    outs = [[] for _ in range(6)]
    for l in range(depth):
        lam_init = 0.8 - 0.6 * math.exp(-0.3 * l)
        wa, wb, wr, wlr = _prep_weights(w_in, gla_w_lr, l)
        blr = gla_b_lr[l][None, :]
        gpre = g_pre[l][None, :]
        gpost = g_post[l][None, :]
        gnorm = gla_norm[l][None, :]
        dnorm = diff_norm[l][None, :]
        lam4 = jnp.stack([lam_q1[l], lam_k1[l], lam_q2[l], lam_k2[l]])
        wpa, wpb, wo = w_pa[l].astype(BF16), w_pb[l].astype(BF16), w_o[l].astype(BF16)

        gq, gk, gv, gg, la, qh, kf, kb, vf, vb, dg, ga, gb = _inproj(
            xp, gpre, wa, wb, wr, wlr, blr, tabs_p, INPROJ_TM, BF16)
        oa, sfin_p = _gla(gq, gk, gv, la, gg, gnorm, None, l, bp, lp, GLA_BLOCK, math.gcd(lp, GLA_CHUNK),
                          BF16, BF16)
        ob = _attn_prompt(qh, kb, vb, dg, lam4, dnorm, bp, lp, ATTN_TQ, ATTN_TK, lam_init)
        xp = _merge(xp, oa, ob, ga, gb, wpa, wpb, wo, gpost, MERGE_TM)
        outs[0].append(kf.reshape(bp, lp, DIFF_KV_HEADS, 2 * DIFF_DH))
        outs[1].append(vf.reshape(bp, lp, DIFF_KV_HEADS, 2 * DIFF_DH))
        outs[2].append(sfin_p)

        gq, gk, gv, gg, la, qh, kf, _, vf, _, dg, ga, gb = _inproj(
            xs, gpre, wa, wb, wr, wlr, blr, tabs_s, bs * ls, F32)
        oa, sfin_s = _gla(gq, gk, gv, la, gg, gnorm, state_gla, l, bs, ls, ls, math.gcd(ls, GLA_CHUNK),
                          F32, F32)
        ob = _attn_sample(page_table, qh, ck, cv, l, kf, vf, dg, lam4, dnorm, ls, SAMPLE_PAGES_PER_STEP,
                          lam_init)
        xs = _merge(xs, oa, ob, ga, gb, wpa, wpb, wo, gpost, bs * ls)
        outs[3].append(kf.reshape(bs, ls, DIFF_KV_HEADS, 2 * DIFF_DH))
        outs[4].append(vf.reshape(bs, ls, DIFF_KV_HEADS, 2 * DIFF_DH))
        outs[5].append(sfin_s)

    return (xp.reshape(bp, lp, D_MODEL), xs.reshape(bs, ls, D_MODEL),
            jnp.stack(outs[0]), jnp.stack(outs[1]), jnp.stack(outs[2]),
            jnp.stack(outs[3]), jnp.stack(outs[4]), jnp.stack(outs[5]))
```

```python
import functools
import math

import jax
import jax.numpy as jnp
import numpy as np
from jax import lax
from jax.experimental import pallas as pl
from jax.experimental.pallas import tpu as pltpu

F32 = jnp.float32
BF16 = jnp.bfloat16

D_MODEL = 1024
GLA_HEADS = 4
GLA_DK = 128
GLA_DV = 256
GLA_RANK = 16
GLA_TAU = 16.0
GLA_CHUNK = 64
DIFF_HEADS = 8
DIFF_KV_HEADS = 4
DIFF_GROUP = 2
DIFF_DH = 64
ROT_DIM = 16
ROPE_THETA = 500000.0
EPS = 1e-6
LOG2E = math.log2(math.e)
LANES = 128
VMEM_LIMIT = 56 * 1024 * 1024

W_GLA_Q = GLA_HEADS * GLA_DK
W_GLA_V = GLA_HEADS * GLA_DV
W_DQ = DIFF_HEADS * 2 * DIFF_DH
W_DK = DIFF_KV_HEADS * 2 * DIFF_DH
Q_SLABS = 2 * DIFF_GROUP

W_GROUP_A = 2 * W_GLA_Q + 2 * W_GLA_V
W_GROUP_B = W_DQ + 2 * W_DK + W_DQ + 2 * D_MODEL
C_GQ = 0
C_GK = C_GQ + W_GLA_Q
C_GV = C_GK + W_GLA_Q
C_GG = C_GV + W_GLA_V
C_DQ = 0
C_DK = C_DQ + W_DQ
C_DV = C_DK + W_DK
C_DG = C_DV + W_DK
C_GA = C_DG + W_DQ
C_GB = C_GA + D_MODEL

INPROJ_TM = 256
GLA_BLOCK = 512
ATTN_TQ = 512
ATTN_TK = 512
MERGE_TM = 512
SAMPLE_PAGES_PER_STEP = 32


def _silu(x):
    return x * jax.nn.sigmoid(x)


def _dot(a, b):
    return jnp.dot(a, b, preferred_element_type=F32)


def _dot_nt(a, b):
    return lax.dot_general(a, b, (((1,), (1,)), ((), ())), preferred_element_type=F32)


def _dot_tn(a, b, precision=None):
    return lax.dot_general(a, b, (((0,), (0,)), ((), ())), preferred_element_type=F32,
                           precision=precision)


def _inproj_kernel(x_ref, gpre_ref, wa_ref, wb_ref, wr_ref, wlr_ref, blr_ref, cos_ref, sup_ref, sdn_ref,
                   gq_ref, gk_ref, gv_ref, gg_ref, la_ref, qh_ref, kf_ref, kb_ref,
                   vf_ref, vb_ref, dg_ref, ga_ref, gb_ref):
    x = x_ref[...]
    ms = jnp.mean(x * x, axis=-1, keepdims=True)
    xn = (x * lax.rsqrt(ms + EPS) * gpre_ref[...]).astype(BF16)

    def proj(w_ref, c0, width):
        return _dot(xn, w_ref[:, c0:c0 + width])

    gq_ref[...] = (proj(wa_ref, C_GQ, W_GLA_Q) * (GLA_DK ** -0.5)).astype(gq_ref.dtype)
    gk_ref[...] = proj(wa_ref, C_GK, W_GLA_Q).astype(gk_ref.dtype)
    gv_ref[...] = proj(wa_ref, C_GV, W_GLA_V).astype(gv_ref.dtype)
    gg_ref[...] = _silu(proj(wa_ref, C_GG, W_GLA_V)).astype(gg_ref.dtype)
    dg_ref[...] = _silu(proj(wb_ref, C_DG, W_DQ)).astype(dg_ref.dtype)
    ga_ref[...] = jax.nn.sigmoid(proj(wb_ref, C_GA, D_MODEL)).astype(ga_ref.dtype)
    gb_ref[...] = jax.nn.sigmoid(proj(wb_ref, C_GB, D_MODEL)).astype(gb_ref.dtype)

    glr = _dot(xn, wr_ref[...]).astype(BF16)
    z = _dot(glr, wlr_ref[...]) + blr_ref[...]
    la_ref[...] = (jnp.minimum(z, 0.0) - jnp.log(1.0 + jnp.exp(-jnp.abs(z)))) * (1.0 / GLA_TAU)

    cos = cos_ref[...]
    sup = sup_ref[...]
    sdn = sdn_ref[...]

    def rope(s):
        return s * cos + pltpu.roll(s, 8, 1) * sup + pltpu.roll(s, LANES - 8, 1) * sdn

    lane = lax.broadcasted_iota(jnp.int32, (1, LANES), 1)
    first = lane < DIFF_DH
    hq = proj(wb_ref, C_DQ, W_DQ)
    for h in range(DIFF_HEADS):
        s = rope(hq[:, h * LANES:(h + 1) * LANES]) * (DIFF_DH ** -0.5 * LOG2E)
        kv, g = divmod(h, DIFF_GROUP)
        qh_ref[kv * Q_SLABS + g] = jnp.where(first, s, 0.0).astype(qh_ref.dtype)
        qh_ref[kv * Q_SLABS + DIFF_GROUP + g] = jnp.where(first, 0.0, s).astype(qh_ref.dtype)
    tm = x.shape[0]
    hk = proj(wb_ref, C_DK, W_DK)
    hv = proj(wb_ref, C_DV, W_DK)
    for j in range(DIFF_KV_HEADS):
        head_rows = pl.ds(j, tm, stride=DIFF_KV_HEADS)
        s = rope(hk[:, j * LANES:(j + 1) * LANES])
        kf_ref[head_rows, :] = s
        kb_ref[:, j * LANES:(j + 1) * LANES] = s.astype(kb_ref.dtype)
        vf_ref[head_rows, :] = hv[:, j * LANES:(j + 1) * LANES]
    vb_ref[...] = hv.astype(vb_ref.dtype)


def _inproj(x2d, gpre, wa, wb, wr, wlr, blr, tabs, tm, act_dtype):
    T = x2d.shape[0]
    n_tab = tabs[0].shape[0] // tm
    row = lambda width: pl.BlockSpec((tm, width), lambda i: (i, 0))
    const = lambda shape: pl.BlockSpec(shape, lambda i: (0,) * len(shape))
    tab = pl.BlockSpec((tm, LANES), lambda i: (i % n_tab, 0))
    cache_rows = pl.BlockSpec((tm * DIFF_KV_HEADS, LANES), lambda i: (i, 0))
    resident = lambda width: pl.BlockSpec((D_MODEL, width), lambda i: (0, 0), pipeline_mode=pl.Buffered(1))
    n_slab = DIFF_KV_HEADS * Q_SLABS
    out_shape = [
        jax.ShapeDtypeStruct((T, W_GLA_Q), act_dtype),
        jax.ShapeDtypeStruct((T, W_GLA_Q), act_dtype),
        jax.ShapeDtypeStruct((T, W_GLA_V), act_dtype),
        jax.ShapeDtypeStruct((T, W_GLA_V), act_dtype),
        jax.ShapeDtypeStruct((T, W_GLA_Q), F32),
        jax.ShapeDtypeStruct((n_slab, T, LANES), act_dtype),
        jax.ShapeDtypeStruct((T * DIFF_KV_HEADS, LANES), F32),
        jax.ShapeDtypeStruct((T, W_DK), act_dtype),
        jax.ShapeDtypeStruct((T * DIFF_KV_HEADS, LANES), F32),
        jax.ShapeDtypeStruct((T, W_DK), act_dtype),
        jax.ShapeDtypeStruct((T, W_DQ), act_dtype),
        jax.ShapeDtypeStruct((T, D_MODEL), act_dtype),
        jax.ShapeDtypeStruct((T, D_MODEL), act_dtype),
    ]
    out_specs = [row(W_GLA_Q), row(W_GLA_Q), row(W_GLA_V), row(W_GLA_V), row(W_GLA_Q),
                 pl.BlockSpec((n_slab, tm, LANES), lambda i: (0, i, 0)),
                 cache_rows, row(W_DK), cache_rows, row(W_DK), row(W_DQ), row(D_MODEL), row(D_MODEL)]
    return pl.pallas_call(
        _inproj_kernel,
        out_shape=out_shape,
        grid=(T // tm,),
        in_specs=[row(D_MODEL), const((1, D_MODEL)),
                  resident(W_GROUP_A), resident(W_GROUP_B), resident(LANES),
                  const((LANES, W_GLA_Q)), const((1, W_GLA_Q)), tab, tab, tab],
        out_specs=out_specs,
        compiler_params=pltpu.CompilerParams(dimension_semantics=("arbitrary",),
                                             vmem_limit_bytes=VMEM_LIMIT),
        name="inproj",
    )(x2d, gpre, wa, wb, wr, wlr, blr, *tabs)


def _gla_kernel(*refs, chunk, n_chunks, has_s0, mxu_dtype):
    if has_s0:
        q_ref, k_ref, v_ref, la_ref, gg_ref, norm_ref, s0_ref, o_ref, sfin_ref, s_scr = refs
    else:
        q_ref, k_ref, v_ref, la_ref, gg_ref, norm_ref, o_ref, sfin_ref, s_scr = refs
    i = pl.program_id(1)

    @pl.when(i == 0)
    def _():
        for h in range(GLA_HEADS):
            if has_s0:
                s_scr[h] = s0_ref[0, h].T
            else:
                s_scr[h] = jnp.zeros(s_scr.shape[1:], F32)

    r = lax.broadcasted_iota(jnp.int32, (chunk, chunk), 0)
    c = lax.broadcasted_iota(jnp.int32, (chunk, chunk), 1)
    causal = r >= c
    norm = norm_ref[...]

    def cumsum_rows(la):
        if chunk < 16:
            return jnp.dot(causal.astype(F32), la, preferred_element_type=F32,
                           precision=lax.Precision.HIGHEST)
        tril = causal.astype(BF16)
        hi = la.astype(BF16)
        lo = (la - hi.astype(F32)).astype(BF16)
        return _dot(tril, hi) + _dot(tril, lo)

    def body(ci, carry):
        sl = pl.ds(pl.multiple_of(ci * chunk, chunk), chunk)
        for h in range(GLA_HEADS):
            kc = slice(h * GLA_DK, (h + 1) * GLA_DK)
            vc = slice(h * GLA_DV, (h + 1) * GLA_DV)
            la = la_ref[sl, kc]
            q = q_ref[sl, kc].astype(F32)
            k = k_ref[sl, kc].astype(F32)
            v = v_ref[sl, vc].astype(mxu_dtype)
            b = cumsum_rows(la)
            bl = b[chunk - 1:chunk, :]
            qt = (q * jnp.exp(b)).astype(mxu_dtype)
            kt = (k * jnp.exp(-b)).astype(mxu_dtype)
            att = jnp.where(causal, _dot_nt(qt, kt), 0.0)
            st_old = s_scr[h]
            o = _dot(att.astype(mxu_dtype), v) + _dot_nt(qt, st_old.astype(mxu_dtype))
            kd = (k * jnp.exp(bl - b)).astype(mxu_dtype)
            s_scr[h] = jnp.exp(bl) * st_old + _dot_tn(v, kd)
            on = o * lax.rsqrt(jnp.mean(o * o, axis=-1, keepdims=True) + EPS) * norm
            o_ref[sl, vc] = (on * gg_ref[sl, vc].astype(F32)).astype(o_ref.dtype)
        return carry

    if n_chunks == 1:
        body(0, 0)
    else:
        lax.fori_loop(0, n_chunks, body, 0, unroll=4)

    @pl.when(i == pl.num_programs(1) - 1)
    def _():
        for h in range(GLA_HEADS):
            sfin_ref[0, h] = s_scr[h].T


def _gla_block_kernel(q_ref, k_ref, v_ref, la_ref, gg_ref, norm_ref, o_ref, sfin_ref, s_scr, *, chunk, n_chunks):
    i = pl.program_id(1)
    lb = chunk * n_chunks

    @pl.when(i == 0)
    def _():
        s_scr[...] = jnp.zeros_like(s_scr)

    r = lax.broadcasted_iota(jnp.int32, (lb, lb), 0)
    c = lax.broadcasted_iota(jnp.int32, (lb, lb), 1)
    mask = jnp.logical_and((r ^ c) < chunk, r >= c)
    tril = mask.astype(BF16)
    norm = norm_ref[...]

    for h in range(GLA_HEADS):
        kc = slice(h * GLA_DK, (h + 1) * GLA_DK)
        vc = slice(h * GLA_DV, (h + 1) * GLA_DV)
        la = la_ref[:, kc]
        hi = la.astype(BF16)
        lo = (la - hi.astype(F32)).astype(BF16)
        b = _dot(tril, hi) + _dot(tril, lo)
        b3 = b.reshape(n_chunks, chunk, GLA_DK)
        bl3 = b3[:, chunk - 1:chunk, :]
        q = q_ref[:, kc].astype(F32)
        k = k_ref[:, kc].astype(F32)
        v = v_ref[:, vc]
        qt = (q * jnp.exp(b)).astype(BF16)
        kt = (k * jnp.exp(-b)).astype(BF16)
        kd = (k.reshape(n_chunks, chunk, GLA_DK) * jnp.exp(bl3 - b3)).reshape(lb, GLA_DK).astype(BF16)
        dec3 = jnp.exp(bl3)
        att = jnp.where(mask, _dot_nt(qt, kt), 0.0).astype(BF16)
        o_intra = _dot(att, v)
        st = s_scr[h]
        outs = []
        for ci in range(n_chunks):
            rows = slice(ci * chunk, (ci + 1) * chunk)
            outs.append(o_intra[rows] + _dot_nt(qt[rows], st.astype(BF16)))
            st = dec3[ci] * st + _dot_tn(v[rows], kd[rows])
        s_scr[h] = st
        o = jnp.concatenate(outs, axis=0)
        on = o * lax.rsqrt(jnp.mean(o * o, axis=-1, keepdims=True) + EPS) * norm
        o_ref[:, vc] = (on * gg_ref[:, vc].astype(F32)).astype(o_ref.dtype)

    @pl.when(i == pl.num_programs(1) - 1)
    def _():
        for h in range(GLA_HEADS):
            sfin_ref[0, h] = s_scr[h].T


def _gla(gq, gk, gv, la, gg, norm, s0, layer, B, L, lb, chunk, act_dtype, mxu_dtype):
    nl = L // lb
    tok = lambda width: pl.BlockSpec((lb, width), lambda b, i: (b * nl + i, 0))
    state_shape = (GLA_HEADS, GLA_DK, GLA_DV)
    in_specs = [tok(W_GLA_Q), tok(W_GLA_Q), tok(W_GLA_V), tok(W_GLA_Q), tok(W_GLA_V),
                pl.BlockSpec((1, GLA_DV), lambda b, i: (0, 0))]
    args = [gq, gk, gv, la, gg, norm]
    if s0 is not None:
        in_specs.append(pl.BlockSpec((None, 1) + state_shape, lambda b, i: (layer, b, 0, 0, 0)))
        args.append(s0)
    if s0 is None:
        body = functools.partial(_gla_block_kernel, chunk=chunk, n_chunks=lb // chunk)
    else:
        body = functools.partial(_gla_kernel, chunk=chunk, n_chunks=lb // chunk, has_s0=True,
                                 mxu_dtype=mxu_dtype)
    return pl.pallas_call(
        body,
        out_shape=[jax.ShapeDtypeStruct((B * L, W_GLA_V), act_dtype),
                   jax.ShapeDtypeStruct((B,) + state_shape, F32)],
        grid=(B, nl),
        in_specs=in_specs,
        out_specs=[tok(W_GLA_V), pl.BlockSpec((1,) + state_shape, lambda b, i: (b, 0, 0, 0))],
        scratch_shapes=[pltpu.VMEM((GLA_HEADS, GLA_DV, GLA_DK), F32)],
        compiler_params=pltpu.CompilerParams(
            dimension_semantics=("parallel", "arbitrary"), vmem_limit_bytes=VMEM_LIMIT),
        name="gla",
    )(*args)


def _lambda(lam_ref, lam_init):
    lv = lam_ref[...]
    a = jnp.sum(lv[0:1] * lv[1:2], axis=1, keepdims=True)
    b = jnp.sum(lv[2:3] * lv[3:4], axis=1, keepdims=True)
    return jnp.exp(a) - jnp.exp(b) + lam_init


def _diff_finalize(acc, l, lam, norm, lam_init):
    half = acc.shape[0] // 2
    o = acc[:half] / l[:half] - lam * (acc[half:] / l[half:])
    return o * lax.rsqrt(jnp.mean(o * o, axis=-1, keepdims=True) + EPS) * norm * (1.0 - lam_init)


def _attn_prompt_kernel(qh_ref, k_ref, v_ref, dg_ref, lam_ref, norm_ref, o_ref,
                        m_scr, l_scr, acc_scr, s_scr, *, tq, tk, lam_init):
    qi = pl.program_id(2)
    n_lane_tiles = tk // LANES

    m_scr[...] = jnp.full_like(m_scr, -jnp.inf)
    l_scr[...] = jnp.zeros_like(l_scr)
    acc_scr[...] = jnp.zeros_like(acc_scr)

    def tile(j, masked, slot):
        s_buf = s_scr.at[slot]
        ks = pl.ds(pl.multiple_of(j * tk, tk), tk)
        k = k_ref[ks, :]
        v = v_ref[ks, :]
        if masked:
            r = lax.broadcasted_iota(jnp.int32, (tq, tk), 0)
            c = lax.broadcasted_iota(jnp.int32, (tq, tk), 1)
            visible = j * tk + c <= qi * tq + r
        s_buf[...] = _dot_nt(qh_ref[...].reshape(Q_SLABS * tq, LANES), k)
        for sl in range(Q_SLABS):
            rows = pl.ds(sl * tq, tq)
            s = s_buf[rows, :]
            if masked:
                s = jnp.where(visible, s, -jnp.inf)
            parts = [s[:, t * LANES:(t + 1) * LANES] for t in range(n_lane_tiles)]
            mx = functools.reduce(jnp.maximum, parts)
            m_old = m_scr[rows, :]
            m_new = jnp.maximum(m_old, jnp.max(mx, axis=-1, keepdims=True))
            alpha = jnp.exp2(m_old - m_new)
            ps = [jnp.exp2(part - m_new) for part in parts]
            l_scr[rows, :] = alpha * l_scr[rows, :] + functools.reduce(jnp.add, ps)
            p = jnp.concatenate(ps, axis=1).astype(BF16)
            acc_scr[rows, :] = alpha * acc_scr[rows, :] + _dot(p, v)
            m_scr[rows, :] = m_new

    n_full = (qi * tq) // tk

    def pair(jj, carry):
        tile(2 * jj, False, 0)
        tile(2 * jj + 1, False, 1)
        return carry

    lax.fori_loop(0, n_full // 2, pair, 0)

    @pl.when(n_full % 2 == 1)
    def _():
        tile(n_full - 1, False, 0)

    tile(n_full, True, 1)

    lam = _lambda(lam_ref, lam_init)
    l = jnp.sum(l_scr[...], axis=-1, keepdims=True)
    o = _diff_finalize(acc_scr[...], l, lam, norm_ref[...], lam_init)
    for g in range(DIFF_GROUP):
        gate = dg_ref[:, g * LANES:(g + 1) * LANES].astype(F32)
        o_ref[:, g * LANES:(g + 1) * LANES] = (o[g * tq:(g + 1) * tq] * gate).astype(o_ref.dtype)


def _attn_prompt(qh, kb, vb, dg, lam4, norm, B, L, tq, tk, lam_init):
    nq = L // tq
    seq = pl.BlockSpec((L, LANES), lambda b, kv, qi: (b, kv))
    return pl.pallas_call(
        functools.partial(_attn_prompt_kernel, tq=tq, tk=tk, lam_init=lam_init),
        out_shape=jax.ShapeDtypeStruct((B * L, W_DQ), BF16),
        grid=(B, DIFF_KV_HEADS, nq),
        in_specs=[pl.BlockSpec((Q_SLABS, tq, LANES), lambda b, kv, qi: (kv, b * nq + qi, 0)),
                  seq, seq,
                  pl.BlockSpec((tq, DIFF_GROUP * LANES), lambda b, kv, qi: (b * nq + qi, kv)),
                  pl.BlockSpec((4, DIFF_DH), lambda b, kv, qi: (0, 0)),
                  pl.BlockSpec((1, LANES), lambda b, kv, qi: (0, 0))],
        out_specs=pl.BlockSpec((tq, DIFF_GROUP * LANES), lambda b, kv, qi: (b * nq + qi, kv)),
        scratch_shapes=[pltpu.VMEM((Q_SLABS * tq, LANES), F32), pltpu.VMEM((Q_SLABS * tq, LANES), F32),
                        pltpu.VMEM((Q_SLABS * tq, LANES), F32), pltpu.VMEM((2, Q_SLABS * tq, tk), F32)],
        compiler_params=pltpu.CompilerParams(
            dimension_semantics=("parallel", "parallel", "arbitrary"),
            vmem_limit_bytes=VMEM_LIMIT),
        name="attn_prompt",
    )(qh, kb, vb, dg, lam4, norm)


def _attn_sample_kernel(pt_ref, qh_ref, *refs, pages, t_new, lam_init):
    k_refs = refs[:pages]
    v_refs = refs[pages:2 * pages]
    kn_ref, vn_ref, dg_ref, lam_ref, norm_ref, o_ref, m_scr, l_scr, acc_scr = refs[2 * pages:]
    p = pl.program_id(1)
    nrow = Q_SLABS * t_new
    page = k_refs[0].shape[0] // DIFF_KV_HEADS

    def q_of(kv):
        return qh_ref[kv * Q_SLABS:(kv + 1) * Q_SLABS].reshape(nrow, LANES)

    @pl.when(p == 0)
    def _():
        r = lax.broadcasted_iota(jnp.int32, (Q_SLABS, t_new, LANES), 1).reshape(nrow, LANES)
        c = lax.broadcasted_iota(jnp.int32, (nrow, LANES), 1)
        pad = jnp.zeros((LANES - t_new, LANES), F32)
        for kv in range(DIFF_KV_HEADS):
            rows = pl.ds(kv * nrow, nrow)
            cols = slice(kv * LANES, (kv + 1) * LANES)
            head_rows = pl.ds(kv, t_new, stride=DIFF_KV_HEADS)
            kn = jnp.concatenate([kn_ref[head_rows, :], pad], axis=0)
            vn = jnp.concatenate([vn_ref[head_rows, :], pad], axis=0)
            s = jnp.where(c <= r, _dot_nt(q_of(kv), kn), -jnp.inf)
            m = jnp.max(s, axis=-1, keepdims=True)
            e = jnp.exp2(s - m)
            m_scr[rows, :] = m
            l_scr[rows, :] = jnp.sum(e, axis=-1, keepdims=True)
            acc_scr[rows, :] = _dot(e, vn)

    s = jnp.concatenate(
        [jnp.concatenate([_dot_nt(q_of(kv), k_refs[j][pl.ds(kv, page, stride=DIFF_KV_HEADS), :])
                          for j in range(pages)], axis=1)
         for kv in range(DIFF_KV_HEADS)], axis=0)
    m_old = m_scr[...]
    m_new = jnp.maximum(m_old, jnp.max(s, axis=-1, keepdims=True))
    alpha = jnp.exp2(m_old - m_new)
    e = jnp.exp2(s - m_new)
    l_scr[...] = alpha * l_scr[...] + jnp.sum(e, axis=-1, keepdims=True)
    m_scr[...] = m_new
    pv = jnp.concatenate(
        [functools.reduce(jnp.add, [
            _dot(e[kv * nrow:(kv + 1) * nrow, j * page:(j + 1) * page],
                 v_refs[j][pl.ds(kv, page, stride=DIFF_KV_HEADS), :]) for j in range(pages)])
         for kv in range(DIFF_KV_HEADS)], axis=0)
    acc_scr[...] = alpha * acc_scr[...] + pv

    @pl.when(p == pl.num_programs(1) - 1)
    def _():
        lam = _lambda(lam_ref, lam_init)
        for kv in range(DIFF_KV_HEADS):
            rows = pl.ds(kv * nrow, nrow)
            o = _diff_finalize(acc_scr[rows, :], l_scr[rows, :], lam, norm_ref[...], lam_init)
            for g in range(DIFF_GROUP):
                h = kv * DIFF_GROUP + g
                gate = dg_ref[:, h * LANES:(h + 1) * LANES]
                o_ref[:, h * LANES:(h + 1) * LANES] = o[g * t_new:(g + 1) * t_new] * gate


def _attn_sample(page_table, qh, cache_k, cache_v, layer, kn, vn, dg, lam4, norm, t_new, pages, lam_init):
    nb, n_pages = page_table.shape
    page_rows = cache_k.shape[2]
    pt = page_table.reshape(-1)

    def page_spec(j):
        return pl.BlockSpec((None, None, page_rows, LANES),
                            lambda b, p, pt_ref: (layer, pt_ref[b * n_pages + p * pages + j], 0, 0))

    tokrow = lambda width: pl.BlockSpec((t_new, width), lambda b, p, pt_ref: (b, 0))
    nrow = Q_SLABS * t_new * DIFF_KV_HEADS
    grid_spec = pltpu.PrefetchScalarGridSpec(
        num_scalar_prefetch=1,
        grid=(nb, n_pages // pages),
        in_specs=[pl.BlockSpec((DIFF_KV_HEADS * Q_SLABS, t_new, LANES), lambda b, p, pt_ref: (0, b, 0))]
                 + [page_spec(j) for j in range(pages)] * 2
                 + [pl.BlockSpec((t_new * DIFF_KV_HEADS, LANES), lambda b, p, pt_ref: (b, 0))] * 2
                 + [tokrow(W_DQ),
                    pl.BlockSpec((4, DIFF_DH), lambda b, p, pt_ref: (0, 0)),
                    pl.BlockSpec((1, LANES), lambda b, p, pt_ref: (0, 0))],
        out_specs=tokrow(W_DQ),
        scratch_shapes=[pltpu.VMEM((nrow, 1), F32), pltpu.VMEM((nrow, 1), F32),
                        pltpu.VMEM((nrow, LANES), F32)],
    )
    return pl.pallas_call(
        functools.partial(_attn_sample_kernel, pages=pages, t_new=t_new, lam_init=lam_init),
        out_shape=jax.ShapeDtypeStruct((nb * t_new, W_DQ), F32),
        grid_spec=grid_spec,
        compiler_params=pltpu.CompilerParams(dimension_semantics=("parallel", "arbitrary"),
                                             vmem_limit_bytes=VMEM_LIMIT),
        name="attn_sample",
    )(pt, qh, *([cache_k] * pages), *([cache_v] * pages), kn, vn, dg, lam4, norm)


def _merge_kernel(x_ref, oa_ref, ob_ref, ga_ref, gb_ref, wpa_ref, wpb_ref, wo_ref, gpost_ref, y_ref):
    pa = _dot(oa_ref[...].astype(BF16), wpa_ref[...])
    pb = _dot(ob_ref[...].astype(BF16), wpb_ref[...])
    merged = ga_ref[...].astype(F32) * pa + gb_ref[...].astype(F32) * pb
    y = _dot(merged.astype(BF16), wo_ref[...])
    yn = y * lax.rsqrt(jnp.mean(y * y, axis=-1, keepdims=True) + EPS) * gpost_ref[...]
    y_ref[...] = x_ref[...] + yn


def _merge(x2d, oa, ob, ga, gb, wpa, wpb, wo, gpost, tm):
    T = x2d.shape[0]
    row = pl.BlockSpec((tm, D_MODEL), lambda i: (i, 0))
    wspec = pl.BlockSpec((D_MODEL, D_MODEL), lambda i: (0, 0))
    return pl.pallas_call(
        _merge_kernel,
        out_shape=jax.ShapeDtypeStruct((T, D_MODEL), F32),
        grid=(T // tm,),
        in_specs=[row, row, row, row, row, wspec, wspec, wspec,
                  pl.BlockSpec((1, D_MODEL), lambda i: (0, 0))],
        out_specs=row,
        compiler_params=pltpu.CompilerParams(dimension_semantics=("parallel",),
                                             vmem_limit_bytes=VMEM_LIMIT),
        name="merge",
    )(x2d, oa, ob, ga, gb, wpa, wpb, wo, gpost)


def _rope_tables(pos):
    half = ROT_DIM // 2
    lane = np.arange(LANES) % DIFF_DH
    inv = ROPE_THETA ** (-(jnp.arange(half, dtype=F32) * 2.0) / ROT_DIM)
    ang = pos.astype(F32)[:, None] * inv[None, :]
    cos_l = jnp.cos(ang)[:, lane % half]
    sin_l = jnp.sin(ang)[:, lane % half]
    rot = jnp.asarray(lane < ROT_DIM)[None, :]
    upper = jnp.asarray((lane >= half) & (lane < ROT_DIM))[None, :]
    lower = jnp.asarray(lane < half)[None, :]
    return (jnp.where(rot, cos_l, 1.0), jnp.where(upper, sin_l, 0.0), jnp.where(lower, -sin_l, 0.0))


def _prep_weights(w_in, gla_w_lr, layer):
    wa = w_in[layer, :, :W_GROUP_A].astype(BF16)
    wb = w_in[layer, :, W_GROUP_A + GLA_RANK:].astype(BF16)
    wr = jnp.pad(w_in[layer, :, W_GROUP_A:W_GROUP_A + GLA_RANK], ((0, 0), (0, LANES - GLA_RANK))).astype(BF16)
    wlr = jnp.pad(gla_w_lr[layer], ((0, LANES - GLA_RANK), (0, 0))).astype(BF16)
    return wa, wb, wr, wlr


def kernel(x_prompt, x_sample, cache_k, cache_v, state_gla, page_table, w_in, gla_w_lr, gla_b_lr,
           gla_norm, lam_q1, lam_k1, lam_q2, lam_k2, diff_norm, w_pa, w_pb, w_o, g_pre, g_post):
    depth = w_in.shape[0]
    bp, lp, _ = x_prompt.shape
    bs, ls, _ = x_sample.shape
    n_pool, page = cache_k.shape[1], cache_k.shape[2]
    past = page_table.shape[1] * page
    ck = cache_k.reshape(depth, n_pool, page * DIFF_KV_HEADS, LANES)
    cv = cache_v.reshape(depth, n_pool, page * DIFF_KV_HEADS, LANES)

    tabs_p = _rope_tables(jnp.arange(lp, dtype=jnp.int32))
    tabs_s = _rope_tables(jnp.tile(past + jnp.arange(ls, dtype=jnp.int32), bs))

    xp = x_prompt.reshape(bp * lp, D_MODEL)
    xs = x_sample.reshape(bs * ls, D_MODEL)
    outs = [[] for _ in range(6)]
    for l in range(depth):
        lam_init = 0.8 - 0.6 * math.exp(-0.3 * l)
        wa, wb, wr, wlr = _prep_weights(w_in, gla_w_lr, l)
        blr = gla_b_lr[l][None, :]
        gpre = g_pre[l][None, :]
        gpost = g_post[l][None, :]
        gnorm = gla_norm[l][None, :]
        dnorm = diff_norm[l][None, :]
        lam4 = jnp.stack([lam_q1[l], lam_k1[l], lam_q2[l], lam_k2[l]])
        wpa, wpb, wo = w_pa[l].astype(BF16), w_pb[l].astype(BF16), w_o[l].astype(BF16)

        gq, gk, gv, gg, la, qh, kf, kb, vf, vb, dg, ga, gb = _inproj(
            xp, gpre, wa, wb, wr, wlr, blr, tabs_p, INPROJ_TM, BF16)
        oa, sfin_p = _gla(gq, gk, gv, la, gg, gnorm, None, l, bp, lp, GLA_BLOCK, math.gcd(lp, GLA_CHUNK),
                          BF16, BF16)
        ob = _attn_prompt(qh, kb, vb, dg, lam4, dnorm, bp, lp, ATTN_TQ, ATTN_TK, lam_init)
        xp = _merge(xp, oa, ob, ga, gb, wpa, wpb, wo, gpost, MERGE_TM)
        outs[0].append(kf.reshape(bp, lp, DIFF_KV_HEADS, 2 * DIFF_DH))
        outs[1].append(vf.reshape(bp, lp, DIFF_KV_HEADS, 2 * DIFF_DH))
        outs[2].append(sfin_p)

        gq, gk, gv, gg, la, qh, kf, _, vf, _, dg, ga, gb = _inproj(
            xs, gpre, wa, wb, wr, wlr, blr, tabs_s, bs * ls, F32)
        oa, sfin_s = _gla(gq, gk, gv, la, gg, gnorm, state_gla, l, bs, ls, ls, math.gcd(ls, GLA_CHUNK),
                          F32, F32)
        ob = _attn_sample(page_table, qh, ck, cv, l, kf, vf, dg, lam4, dnorm, ls, SAMPLE_PAGES_PER_STEP,
                          lam_init)
        xs = _merge(xs, oa, ob, ga, gb, wpa, wpb, wo, gpost, bs * ls)
        outs[3].append(kf.reshape(bs, ls, DIFF_KV_HEADS, 2 * DIFF_DH))
        outs[4].append(vf.reshape(bs, ls, DIFF_KV_HEADS, 2 * DIFF_DH))
        outs[5].append(sfin_s)

    return (xp.reshape(bp, lp, D_MODEL), xs.reshape(bs, ls, D_MODEL),
            jnp.stack(outs[0]), jnp.stack(outs[1]), jnp.stack(outs[2]),
            jnp.stack(outs[3]), jnp.stack(outs[4]), jnp.stack(outs[5]))
```

```python
import functools
import math

import jax
import jax.numpy as jnp
import numpy as np
from jax import lax
from jax.experimental import pallas as pl
from jax.experimental.pallas import tpu as pltpu

F32 = jnp.float32
BF16 = jnp.bfloat16

D_MODEL = 1024
GLA_HEADS = 4
GLA_DK = 128
GLA_DV = 256
GLA_RANK = 16
GLA_TAU = 16.0
GLA_CHUNK = 64
DIFF_HEADS = 8
DIFF_KV_HEADS = 4
DIFF_GROUP = 2
DIFF_DH = 64
ROT_DIM = 16
ROPE_THETA = 500000.0
EPS = 1e-6
LOG2E = math.log2(math.e)
LANES = 128
VMEM_LIMIT = 56 * 1024 * 1024

W_GLA_Q = GLA_HEADS * GLA_DK
W_GLA_V = GLA_HEADS * GLA_DV
W_DQ = DIFF_HEADS * 2 * DIFF_DH
W_DK = DIFF_KV_HEADS * 2 * DIFF_DH
Q_SLABS = 2 * DIFF_GROUP

W_GROUP_A = 2 * W_GLA_Q + 2 * W_GLA_V
W_GROUP_B = W_DQ + 2 * W_DK + W_DQ + 2 * D_MODEL
C_GQ = 0
C_GK = C_GQ + W_GLA_Q
C_GV = C_GK + W_GLA_Q
C_GG = C_GV + W_GLA_V
C_DQ = 0
C_DK = C_DQ + W_DQ
C_DV = C_DK + W_DK
C_DG = C_DV + W_DK
C_GA = C_DG + W_DQ
C_GB = C_GA + D_MODEL

INPROJ_TM = 256
GLA_BLOCK = 512
ATTN_TQ = 512
ATTN_TK = 512
MERGE_TM = 512
SAMPLE_PAGES_PER_STEP = 32


def _silu(x):
    return x * jax.nn.sigmoid(x)


def _dot(a, b):
    return jnp.dot(a, b, preferred_element_type=F32)


def _dot_nt(a, b):
    return lax.dot_general(a, b, (((1,), (1,)), ((), ())), preferred_element_type=F32)


def _dot_tn(a, b, precision=None):
    return lax.dot_general(a, b, (((0,), (0,)), ((), ())), preferred_element_type=F32,
                           precision=precision)


KF_OUT, VF_OUT = 6, 8


def _inproj_kernel(x_ref, gpre_ref, wa_ref, wb_ref, wr_ref, wlr_ref, blr_ref, cos_ref, sup_ref, sdn_ref,
                   *refs, has_prev):
    (gq_ref, gk_ref, gv_ref, gg_ref, la_ref, qh_ref, kf_ref, kb_ref,
     vf_ref, vb_ref, dg_ref, ga_ref, gb_ref) = refs[2:] if has_prev else refs
    x = x_ref[...]
    ms = jnp.mean(x * x, axis=-1, keepdims=True)
    xn = (x * lax.rsqrt(ms + EPS) * gpre_ref[...]).astype(BF16)

    def proj(w_ref, c0, width):
        return _dot(xn, w_ref[:, c0:c0 + width])

    gq_ref[...] = (proj(wa_ref, C_GQ, W_GLA_Q) * (GLA_DK ** -0.5)).astype(gq_ref.dtype)
    gk_ref[...] = proj(wa_ref, C_GK, W_GLA_Q).astype(gk_ref.dtype)
    gv_ref[...] = proj(wa_ref, C_GV, W_GLA_V).astype(gv_ref.dtype)
    gg_ref[...] = _silu(proj(wa_ref, C_GG, W_GLA_V)).astype(gg_ref.dtype)
    dg_ref[...] = _silu(proj(wb_ref, C_DG, W_DQ)).astype(dg_ref.dtype)
    ga_ref[...] = jax.nn.sigmoid(proj(wb_ref, C_GA, D_MODEL)).astype(ga_ref.dtype)
    gb_ref[...] = jax.nn.sigmoid(proj(wb_ref, C_GB, D_MODEL)).astype(gb_ref.dtype)

    glr = _dot(xn, wr_ref[...]).astype(BF16)
    z = _dot(glr, wlr_ref[...]) + blr_ref[...]
    la_ref[...] = (jnp.minimum(z, 0.0) - jnp.log(1.0 + jnp.exp(-jnp.abs(z)))) * (1.0 / GLA_TAU)

    cos = cos_ref[...]
    sup = sup_ref[...]
    sdn = sdn_ref[...]

    def rope(s):
        return s * cos + pltpu.roll(s, 8, 1) * sup + pltpu.roll(s, LANES - 8, 1) * sdn

    lane = lax.broadcasted_iota(jnp.int32, (1, LANES), 1)
    first = lane < DIFF_DH
    hq = proj(wb_ref, C_DQ, W_DQ)
    for h in range(DIFF_HEADS):
        s = rope(hq[:, h * LANES:(h + 1) * LANES]) * (DIFF_DH ** -0.5 * LOG2E)
        kv, g = divmod(h, DIFF_GROUP)
        qh_ref[kv * Q_SLABS + g] = jnp.where(first, s, 0.0).astype(qh_ref.dtype)
        qh_ref[kv * Q_SLABS + DIFF_GROUP + g] = jnp.where(first, 0.0, s).astype(qh_ref.dtype)
    tm = x.shape[0]
    hk = proj(wb_ref, C_DK, W_DK)
    hv = proj(wb_ref, C_DV, W_DK)
    for j in range(DIFF_KV_HEADS):
        head_rows = pl.ds(j, tm, stride=DIFF_KV_HEADS)
        s = rope(hk[:, j * LANES:(j + 1) * LANES])
        kf_ref[head_rows, :] = s
        kb_ref[:, j * LANES:(j + 1) * LANES] = s.astype(kb_ref.dtype)
        vf_ref[head_rows, :] = hv[:, j * LANES:(j + 1) * LANES]
    vb_ref[...] = hv.astype(vb_ref.dtype)


def _inproj(x2d, gpre, wa, wb, wr, wlr, blr, tabs, tm, act_dtype, layer=0, depth=1, kv_prev=None):
    T = x2d.shape[0]
    n_tiles = T // tm
    n_tab = tabs[0].shape[0] // tm
    row = lambda width: pl.BlockSpec((tm, width), lambda i: (i, 0))
    const = lambda shape: pl.BlockSpec(shape, lambda i: (0,) * len(shape))
    tab = pl.BlockSpec((tm, LANES), lambda i: (i % n_tab, 0))
    cache_rows = pl.BlockSpec((tm * DIFF_KV_HEADS, LANES), lambda i: (layer * n_tiles + i, 0))
    resident = lambda width: pl.BlockSpec((D_MODEL, width), lambda i: (0, 0), pipeline_mode=pl.Buffered(1))
    n_slab = DIFF_KV_HEADS * Q_SLABS
    out_shape = [
        jax.ShapeDtypeStruct((T, W_GLA_Q), act_dtype),
        jax.ShapeDtypeStruct((T, W_GLA_Q), act_dtype),
        jax.ShapeDtypeStruct((T, W_GLA_V), act_dtype),
        jax.ShapeDtypeStruct((T, W_GLA_V), act_dtype),
        jax.ShapeDtypeStruct((T, W_GLA_Q), F32),
        jax.ShapeDtypeStruct((n_slab, T, LANES), act_dtype),
        jax.ShapeDtypeStruct((depth * T * DIFF_KV_HEADS, LANES), F32),
        jax.ShapeDtypeStruct((T, W_DK), act_dtype),
        jax.ShapeDtypeStruct((depth * T * DIFF_KV_HEADS, LANES), F32),
        jax.ShapeDtypeStruct((T, W_DK), act_dtype),
        jax.ShapeDtypeStruct((T, W_DQ), act_dtype),
        jax.ShapeDtypeStruct((T, D_MODEL), act_dtype),
        jax.ShapeDtypeStruct((T, D_MODEL), act_dtype),
    ]
    out_specs = [row(W_GLA_Q), row(W_GLA_Q), row(W_GLA_V), row(W_GLA_V), row(W_GLA_Q),
                 pl.BlockSpec((n_slab, tm, LANES), lambda i: (0, i, 0)),
                 cache_rows, row(W_DK), cache_rows, row(W_DK), row(W_DQ), row(D_MODEL), row(D_MODEL)]
    in_specs = [row(D_MODEL), const((1, D_MODEL)),
                resident(W_GROUP_A), resident(W_GROUP_B), resident(LANES),
                const((LANES, W_GLA_Q)), const((1, W_GLA_Q)), tab, tab, tab]
    args = [x2d, gpre, wa, wb, wr, wlr, blr, *tabs]
    aliases = {}
    if kv_prev is not None:
        aliases = {len(args): KF_OUT, len(args) + 1: VF_OUT}
        in_specs += [pl.BlockSpec(memory_space=pl.ANY)] * 2
        args += list(kv_prev)
    return pl.pallas_call(
        functools.partial(_inproj_kernel, has_prev=kv_prev is not None),
        out_shape=out_shape,
        grid=(n_tiles,),
        in_specs=in_specs,
        out_specs=out_specs,
        input_output_aliases=aliases,
        compiler_params=pltpu.CompilerParams(dimension_semantics=("arbitrary",),
                                             vmem_limit_bytes=VMEM_LIMIT),
        name="inproj",
    )(*args)


def _gla_kernel(*refs, chunk, n_chunks, has_s0, mxu_dtype):
    if has_s0:
        q_ref, k_ref, v_ref, la_ref, gg_ref, norm_ref, s0_ref, o_ref, sfin_ref, s_scr = refs
    else:
        q_ref, k_ref, v_ref, la_ref, gg_ref, norm_ref, o_ref, sfin_ref, s_scr = refs
    i = pl.program_id(1)

    @pl.when(i == 0)
    def _():
        for h in range(GLA_HEADS):
            if has_s0:
                s_scr[h] = s0_ref[0, h].T
            else:
                s_scr[h] = jnp.zeros(s_scr.shape[1:], F32)

    r = lax.broadcasted_iota(jnp.int32, (chunk, chunk), 0)
    c = lax.broadcasted_iota(jnp.int32, (chunk, chunk), 1)
    causal = r >= c
    norm = norm_ref[...]

    def cumsum_rows(la):
        if chunk < 16:
            return jnp.dot(causal.astype(F32), la, preferred_element_type=F32,
                           precision=lax.Precision.HIGHEST)
        tril = causal.astype(BF16)
        hi = la.astype(BF16)
        lo = (la - hi.astype(F32)).astype(BF16)
        return _dot(tril, hi) + _dot(tril, lo)

    def body(ci, carry):
        sl = pl.ds(pl.multiple_of(ci * chunk, chunk), chunk)
        for h in range(GLA_HEADS):
            kc = slice(h * GLA_DK, (h + 1) * GLA_DK)
            vc = slice(h * GLA_DV, (h + 1) * GLA_DV)
            la = la_ref[sl, kc]
            q = q_ref[sl, kc].astype(F32)
            k = k_ref[sl, kc].astype(F32)
            v = v_ref[sl, vc].astype(mxu_dtype)
            b = cumsum_rows(la)
            bl = b[chunk - 1:chunk, :]
            qt = (q * jnp.exp(b)).astype(mxu_dtype)
            kt = (k * jnp.exp(-b)).astype(mxu_dtype)
            att = jnp.where(causal, _dot_nt(qt, kt), 0.0)
            st_old = s_scr[h]
            o = _dot(att.astype(mxu_dtype), v) + _dot_nt(qt, st_old.astype(mxu_dtype))
            kd = (k * jnp.exp(bl - b)).astype(mxu_dtype)
            s_scr[h] = jnp.exp(bl) * st_old + _dot_tn(v, kd)
            on = o * lax.rsqrt(jnp.mean(o * o, axis=-1, keepdims=True) + EPS) * norm
            o_ref[sl, vc] = (on * gg_ref[sl, vc].astype(F32)).astype(o_ref.dtype)
        return carry

    if n_chunks == 1:
        body(0, 0)
    else:
        lax.fori_loop(0, n_chunks, body, 0, unroll=4)

    @pl.when(i == pl.num_programs(1) - 1)
    def _():
        for h in range(GLA_HEADS):
            sfin_ref[0, h] = s_scr[h].T


def _gla_block_kernel(q_ref, k_ref, v_ref, la_ref, gg_ref, norm_ref, o_ref, sfin_ref, s_scr, *, chunk, n_chunks):
    i = pl.program_id(1)
    lb = chunk * n_chunks

    @pl.when(i == 0)
    def _():
        s_scr[...] = jnp.zeros_like(s_scr)

    r = lax.broadcasted_iota(jnp.int32, (lb, lb), 0)
    c = lax.broadcasted_iota(jnp.int32, (lb, lb), 1)
    mask = jnp.logical_and((r ^ c) < chunk, r >= c)
    tril = mask.astype(BF16)
    norm = norm_ref[...]

    for h in range(GLA_HEADS):
        kc = slice(h * GLA_DK, (h + 1) * GLA_DK)
        vc = slice(h * GLA_DV, (h + 1) * GLA_DV)
        la = la_ref[:, kc]
        hi = la.astype(BF16)
        lo = (la - hi.astype(F32)).astype(BF16)
        b = _dot(tril, hi) + _dot(tril, lo)
        b3 = b.reshape(n_chunks, chunk, GLA_DK)
        bl3 = b3[:, chunk - 1:chunk, :]
        q = q_ref[:, kc].astype(F32)
        k = k_ref[:, kc].astype(F32)
        v = v_ref[:, vc]
        qt = (q * jnp.exp(b)).astype(BF16)
        kt = (k * jnp.exp(-b)).astype(BF16)
        kd = (k.reshape(n_chunks, chunk, GLA_DK) * jnp.exp(bl3 - b3)).reshape(lb, GLA_DK).astype(BF16)
        dec3 = jnp.exp(bl3)
        att = jnp.where(mask, _dot_nt(qt, kt), 0.0).astype(BF16)
        o_intra = _dot(att, v)
        st = s_scr[h]
        outs = []
        for ci in range(n_chunks):
            rows = slice(ci * chunk, (ci + 1) * chunk)
            outs.append(o_intra[rows] + _dot_nt(qt[rows], st.astype(BF16)))
            st = dec3[ci] * st + _dot_tn(v[rows], kd[rows])
        s_scr[h] = st
        o = jnp.concatenate(outs, axis=0)
        on = o * lax.rsqrt(jnp.mean(o * o, axis=-1, keepdims=True) + EPS) * norm
        o_ref[:, vc] = (on * gg_ref[:, vc].astype(F32)).astype(o_ref.dtype)

    @pl.when(i == pl.num_programs(1) - 1)
    def _():
        for h in range(GLA_HEADS):
            sfin_ref[0, h] = s_scr[h].T


def _gla(gq, gk, gv, la, gg, norm, s0, layer, B, L, lb, chunk, act_dtype, mxu_dtype):
    nl = L // lb
    tok = lambda width: pl.BlockSpec((lb, width), lambda b, i: (b * nl + i, 0))
    state_shape = (GLA_HEADS, GLA_DK, GLA_DV)
    in_specs = [tok(W_GLA_Q), tok(W_GLA_Q), tok(W_GLA_V), tok(W_GLA_Q), tok(W_GLA_V),
                pl.BlockSpec((1, GLA_DV), lambda b, i: (0, 0))]
    args = [gq, gk, gv, la, gg, norm]
    if s0 is not None:
        in_specs.append(pl.BlockSpec((None, 1) + state_shape, lambda b, i: (layer, b, 0, 0, 0)))
        args.append(s0)
    if s0 is None:
        body = functools.partial(_gla_block_kernel, chunk=chunk, n_chunks=lb // chunk)
    else:
        body = functools.partial(_gla_kernel, chunk=chunk, n_chunks=lb // chunk, has_s0=True,
                                 mxu_dtype=mxu_dtype)
    return pl.pallas_call(
        body,
        out_shape=[jax.ShapeDtypeStruct((B * L, W_GLA_V), act_dtype),
                   jax.ShapeDtypeStruct((B,) + state_shape, F32)],
        grid=(B, nl),
        in_specs=in_specs,
        out_specs=[tok(W_GLA_V), pl.BlockSpec((1,) + state_shape, lambda b, i: (b, 0, 0, 0))],
        scratch_shapes=[pltpu.VMEM((GLA_HEADS, GLA_DV, GLA_DK), F32)],
        compiler_params=pltpu.CompilerParams(
            dimension_semantics=("parallel", "arbitrary"), vmem_limit_bytes=VMEM_LIMIT),
        name="gla",
    )(*args)


def _lambda(lam_ref, lam_init):
    lv = lam_ref[...]
    a = jnp.sum(lv[0:1] * lv[1:2], axis=1, keepdims=True)
    b = jnp.sum(lv[2:3] * lv[3:4], axis=1, keepdims=True)
    return jnp.exp(a) - jnp.exp(b) + lam_init


def _diff_finalize(acc, l, lam, norm, lam_init):
    half = acc.shape[0] // 2
    o = acc[:half] / l[:half] - lam * (acc[half:] / l[half:])
    return o * lax.rsqrt(jnp.mean(o * o, axis=-1, keepdims=True) + EPS) * norm * (1.0 - lam_init)


def _attn_prompt_kernel(qh_ref, k_ref, v_ref, dg_ref, lam_ref, norm_ref, o_ref,
                        m_scr, l_scr, acc_scr, s_scr, *, tq, tk, lam_init):
    qi = pl.program_id(2)
    n_lane_tiles = tk // LANES

    m_scr[...] = jnp.full_like(m_scr, -jnp.inf)
    l_scr[...] = jnp.zeros_like(l_scr)
    acc_scr[...] = jnp.zeros_like(acc_scr)

    def tile(j, masked, slot):
        s_buf = s_scr.at[slot]
        ks = pl.ds(pl.multiple_of(j * tk, tk), tk)
        k = k_ref[ks, :]
        v = v_ref[ks, :]
        if masked:
            r = lax.broadcasted_iota(jnp.int32, (tq, tk), 0)
            c = lax.broadcasted_iota(jnp.int32, (tq, tk), 1)
            visible = j * tk + c <= qi * tq + r
        s_buf[...] = _dot_nt(qh_ref[...].reshape(Q_SLABS * tq, LANES), k)
        for sl in range(Q_SLABS):
            rows = pl.ds(sl * tq, tq)
            s = s_buf[rows, :]
            if masked:
                s = jnp.where(visible, s, -jnp.inf)
            parts = [s[:, t * LANES:(t + 1) * LANES] for t in range(n_lane_tiles)]
            mx = functools.reduce(jnp.maximum, parts)
            m_old = m_scr[rows, :]
            m_new = jnp.maximum(m_old, jnp.max(mx, axis=-1, keepdims=True))
            alpha = jnp.exp2(m_old - m_new)
            ps = [jnp.exp2(part - m_new) for part in parts]
            l_scr[rows, :] = alpha * l_scr[rows, :] + functools.reduce(jnp.add, ps)
            p = jnp.concatenate(ps, axis=1).astype(BF16)
            acc_scr[rows, :] = alpha * acc_scr[rows, :] + _dot(p, v)
            m_scr[rows, :] = m_new

    n_full = (qi * tq) // tk

    def pair(jj, carry):
        tile(2 * jj, False, 0)
        tile(2 * jj + 1, False, 1)
        return carry

    lax.fori_loop(0, n_full // 2, pair, 0)

    @pl.when(n_full % 2 == 1)
    def _():
        tile(n_full - 1, False, 0)

    tile(n_full, True, 1)

    lam = _lambda(lam_ref, lam_init)
    l = jnp.sum(l_scr[...], axis=-1, keepdims=True)
    o = _diff_finalize(acc_scr[...], l, lam, norm_ref[...], lam_init)
    for g in range(DIFF_GROUP):
        gate = dg_ref[:, g * LANES:(g + 1) * LANES].astype(F32)
        o_ref[:, g * LANES:(g + 1) * LANES] = (o[g * tq:(g + 1) * tq] * gate).astype(o_ref.dtype)


def _attn_prompt(qh, kb, vb, dg, lam4, norm, B, L, tq, tk, lam_init):
    nq = L // tq
    seq = pl.BlockSpec((L, LANES), lambda b, kv, qi: (b, kv))
    return pl.pallas_call(
        functools.partial(_attn_prompt_kernel, tq=tq, tk=tk, lam_init=lam_init),
        out_shape=jax.ShapeDtypeStruct((B * L, W_DQ), BF16),
        grid=(B, DIFF_KV_HEADS, nq),
        in_specs=[pl.BlockSpec((Q_SLABS, tq, LANES), lambda b, kv, qi: (kv, b * nq + qi, 0)),
                  seq, seq,
                  pl.BlockSpec((tq, DIFF_GROUP * LANES), lambda b, kv, qi: (b * nq + qi, kv)),
                  pl.BlockSpec((4, DIFF_DH), lambda b, kv, qi: (0, 0)),
                  pl.BlockSpec((1, LANES), lambda b, kv, qi: (0, 0))],
        out_specs=pl.BlockSpec((tq, DIFF_GROUP * LANES), lambda b, kv, qi: (b * nq + qi, kv)),
        scratch_shapes=[pltpu.VMEM((Q_SLABS * tq, LANES), F32), pltpu.VMEM((Q_SLABS * tq, LANES), F32),
                        pltpu.VMEM((Q_SLABS * tq, LANES), F32), pltpu.VMEM((2, Q_SLABS * tq, tk), F32)],
        compiler_params=pltpu.CompilerParams(
            dimension_semantics=("parallel", "parallel", "arbitrary"),
            vmem_limit_bytes=VMEM_LIMIT),
        name="attn_prompt",
    )(qh, kb, vb, dg, lam4, norm)


def _attn_sample_kernel(pt_ref, qh_ref, *refs, pages, t_new, lam_init):
    k_refs = refs[:pages]
    v_refs = refs[pages:2 * pages]
    kn_ref, vn_ref, dg_ref, lam_ref, norm_ref, o_ref, m_scr, l_scr, acc_scr = refs[2 * pages:]
    p = pl.program_id(1)
    nrow = Q_SLABS * t_new
    page = k_refs[0].shape[0] // DIFF_KV_HEADS

    def q_of(kv):
        return qh_ref[kv * Q_SLABS:(kv + 1) * Q_SLABS].reshape(nrow, LANES)

    @pl.when(p == 0)
    def _():
        r = lax.broadcasted_iota(jnp.int32, (Q_SLABS, t_new, LANES), 1).reshape(nrow, LANES)
        c = lax.broadcasted_iota(jnp.int32, (nrow, LANES), 1)
        pad = jnp.zeros((LANES - t_new, LANES), F32)
        for kv in range(DIFF_KV_HEADS):
            rows = pl.ds(kv * nrow, nrow)
            cols = slice(kv * LANES, (kv + 1) * LANES)
            head_rows = pl.ds(kv, t_new, stride=DIFF_KV_HEADS)
            kn = jnp.concatenate([kn_ref[head_rows, :], pad], axis=0)
            vn = jnp.concatenate([vn_ref[head_rows, :], pad], axis=0)
            s = jnp.where(c <= r, _dot_nt(q_of(kv), kn), -jnp.inf)
            m = jnp.max(s, axis=-1, keepdims=True)
            e = jnp.exp2(s - m)
            m_scr[rows, :] = m
            l_scr[rows, :] = jnp.sum(e, axis=-1, keepdims=True)
            acc_scr[rows, :] = _dot(e, vn)

    s = jnp.concatenate(
        [jnp.concatenate([_dot_nt(q_of(kv), k_refs[j][pl.ds(kv, page, stride=DIFF_KV_HEADS), :])
                          for j in range(pages)], axis=1)
         for kv in range(DIFF_KV_HEADS)], axis=0)
    m_old = m_scr[...]
    m_new = jnp.maximum(m_old, jnp.max(s, axis=-1, keepdims=True))
    alpha = jnp.exp2(m_old - m_new)
    e = jnp.exp2(s - m_new)
    l_scr[...] = alpha * l_scr[...] + jnp.sum(e, axis=-1, keepdims=True)
    m_scr[...] = m_new
    pv = jnp.concatenate(
        [functools.reduce(jnp.add, [
            _dot(e[kv * nrow:(kv + 1) * nrow, j * page:(j + 1) * page],
                 v_refs[j][pl.ds(kv, page, stride=DIFF_KV_HEADS), :]) for j in range(pages)])
         for kv in range(DIFF_KV_HEADS)], axis=0)
    acc_scr[...] = alpha * acc_scr[...] + pv

    @pl.when(p == pl.num_programs(1) - 1)
    def _():
        lam = _lambda(lam_ref, lam_init)
        for kv in range(DIFF_KV_HEADS):
            rows = pl.ds(kv * nrow, nrow)
            o = _diff_finalize(acc_scr[rows, :], l_scr[rows, :], lam, norm_ref[...], lam_init)
            for g in range(DIFF_GROUP):
                h = kv * DIFF_GROUP + g
                gate = dg_ref[:, h * LANES:(h + 1) * LANES]
                o_ref[:, h * LANES:(h + 1) * LANES] = o[g * t_new:(g + 1) * t_new] * gate


def _attn_sample(page_table, qh, cache_k, cache_v, layer, kn, vn, dg, lam4, norm, t_new, pages, lam_init):
    nb, n_pages = page_table.shape
    page_rows = cache_k.shape[2]
    pt = page_table.reshape(-1)

    def page_spec(j):
        return pl.BlockSpec((None, None, page_rows, LANES),
                            lambda b, p, pt_ref: (layer, pt_ref[b * n_pages + p * pages + j], 0, 0))

    tokrow = lambda width: pl.BlockSpec((t_new, width), lambda b, p, pt_ref: (b, 0))
    nrow = Q_SLABS * t_new * DIFF_KV_HEADS
    grid_spec = pltpu.PrefetchScalarGridSpec(
        num_scalar_prefetch=1,
        grid=(nb, n_pages // pages),
        in_specs=[pl.BlockSpec((DIFF_KV_HEADS * Q_SLABS, t_new, LANES), lambda b, p, pt_ref: (0, b, 0))]
                 + [page_spec(j) for j in range(pages)] * 2
                 + [pl.BlockSpec((t_new * DIFF_KV_HEADS, LANES), lambda b, p, pt_ref: (b, 0))] * 2
                 + [tokrow(W_DQ),
                    pl.BlockSpec((4, DIFF_DH), lambda b, p, pt_ref: (0, 0)),
                    pl.BlockSpec((1, LANES), lambda b, p, pt_ref: (0, 0))],
        out_specs=tokrow(W_DQ),
        scratch_shapes=[pltpu.VMEM((nrow, 1), F32), pltpu.VMEM((nrow, 1), F32),
                        pltpu.VMEM((nrow, LANES), F32)],
    )
    return pl.pallas_call(
        functools.partial(_attn_sample_kernel, pages=pages, t_new=t_new, lam_init=lam_init),
        out_shape=jax.ShapeDtypeStruct((nb * t_new, W_DQ), F32),
        grid_spec=grid_spec,
        compiler_params=pltpu.CompilerParams(dimension_semantics=("parallel", "arbitrary"),
                                             vmem_limit_bytes=VMEM_LIMIT),
        name="attn_sample",
    )(pt, qh, *([cache_k] * pages), *([cache_v] * pages), kn, vn, dg, lam4, norm)


def _merge_kernel(x_ref, oa_ref, ob_ref, ga_ref, gb_ref, wpa_ref, wpb_ref, wo_ref, gpost_ref, y_ref):
    pa = _dot(oa_ref[...].astype(BF16), wpa_ref[...])
    pb = _dot(ob_ref[...].astype(BF16), wpb_ref[...])
    merged = ga_ref[...].astype(F32) * pa + gb_ref[...].astype(F32) * pb
    y = _dot(merged.astype(BF16), wo_ref[...])
    yn = y * lax.rsqrt(jnp.mean(y * y, axis=-1, keepdims=True) + EPS) * gpost_ref[...]
    y_ref[...] = x_ref[...] + yn


def _merge(x2d, oa, ob, ga, gb, wpa, wpb, wo, gpost, tm):
    T = x2d.shape[0]
    row = pl.BlockSpec((tm, D_MODEL), lambda i: (i, 0))
    wspec = pl.BlockSpec((D_MODEL, D_MODEL), lambda i: (0, 0))
    return pl.pallas_call(
        _merge_kernel,
        out_shape=jax.ShapeDtypeStruct((T, D_MODEL), F32),
        grid=(T // tm,),
        in_specs=[row, row, row, row, row, wspec, wspec, wspec,
                  pl.BlockSpec((1, D_MODEL), lambda i: (0, 0))],
        out_specs=row,
        compiler_params=pltpu.CompilerParams(dimension_semantics=("parallel",),
                                             vmem_limit_bytes=VMEM_LIMIT),
        name="merge",
    )(x2d, oa, ob, ga, gb, wpa, wpb, wo, gpost)


def _rope_tables(pos):
    half = ROT_DIM // 2
    lane = np.arange(LANES) % DIFF_DH
    inv = ROPE_THETA ** (-(jnp.arange(half, dtype=F32) * 2.0) / ROT_DIM)
    ang = pos.astype(F32)[:, None] * inv[None, :]
    cos_l = jnp.cos(ang)[:, lane % half]
    sin_l = jnp.sin(ang)[:, lane % half]
    rot = jnp.asarray(lane < ROT_DIM)[None, :]
    upper = jnp.asarray((lane >= half) & (lane < ROT_DIM))[None, :]
    lower = jnp.asarray(lane < half)[None, :]
    return (jnp.where(rot, cos_l, 1.0), jnp.where(upper, sin_l, 0.0), jnp.where(lower, -sin_l, 0.0))


def _prep_weights(w_in, gla_w_lr, layer):
    wa = w_in[layer, :, :W_GROUP_A].astype(BF16)
    wb = w_in[layer, :, W_GROUP_A + GLA_RANK:].astype(BF16)
    wr = jnp.pad(w_in[layer, :, W_GROUP_A:W_GROUP_A + GLA_RANK], ((0, 0), (0, LANES - GLA_RANK))).astype(BF16)
    wlr = jnp.pad(gla_w_lr[layer], ((0, LANES - GLA_RANK), (0, 0))).astype(BF16)
    return wa, wb, wr, wlr


def kernel(x_prompt, x_sample, cache_k, cache_v, state_gla, page_table, w_in, gla_w_lr, gla_b_lr,
           gla_norm, lam_q1, lam_k1, lam_q2, lam_k2, diff_norm, w_pa, w_pb, w_o, g_pre, g_post):
    depth = w_in.shape[0]
    bp, lp, _ = x_prompt.shape
    bs, ls, _ = x_sample.shape
    n_pool, page = cache_k.shape[1], cache_k.shape[2]
    past = page_table.shape[1] * page
    ck = cache_k.reshape(depth, n_pool, page * DIFF_KV_HEADS, LANES)
    cv = cache_v.reshape(depth, n_pool, page * DIFF_KV_HEADS, LANES)

    tabs_p = _rope_tables(jnp.arange(lp, dtype=jnp.int32))
    tabs_s = _rope_tables(jnp.tile(past + jnp.arange(ls, dtype=jnp.int32), bs))

    xp = x_prompt.reshape(bp * lp, D_MODEL)
    xs = x_sample.reshape(bs * ls, D_MODEL)
    outs = [[] for _ in range(6)]
    kv_prompt = None
    for l in range(depth):
        lam_init = 0.8 - 0.6 * math.exp(-0.3 * l)
        wa, wb, wr, wlr = _prep_weights(w_in, gla_w_lr, l)
        blr = gla_b_lr[l][None, :]
        gpre = g_pre[l][None, :]
        gpost = g_post[l][None, :]
        gnorm = gla_norm[l][None, :]
        dnorm = diff_norm[l][None, :]
        lam4 = jnp.stack([lam_q1[l], lam_k1[l], lam_q2[l], lam_k2[l]])
        wpa, wpb, wo = w_pa[l].astype(BF16), w_pb[l].astype(BF16), w_o[l].astype(BF16)

        gq, gk, gv, gg, la, qh, kf_p, kb, vf_p, vb, dg, ga, gb = _inproj(
            xp, gpre, wa, wb, wr, wlr, blr, tabs_p, INPROJ_TM, BF16, layer=l, depth=depth, kv_prev=kv_prompt)
        kv_prompt = (kf_p, vf_p)
        oa, sfin_p = _gla(gq, gk, gv, la, gg, gnorm, None, l, bp, lp, GLA_BLOCK, math.gcd(lp, GLA_CHUNK),
                          BF16, BF16)
        ob = _attn_prompt(qh, kb, vb, dg, lam4, dnorm, bp, lp, ATTN_TQ, ATTN_TK, lam_init)
        xp = _merge(xp, oa, ob, ga, gb, wpa, wpb, wo, gpost, MERGE_TM)
        outs[2].append(sfin_p)

        gq, gk, gv, gg, la, qh, kf, _, vf, _, dg, ga, gb = _inproj(
            xs, gpre, wa, wb, wr, wlr, blr, tabs_s, bs * ls, F32)
        oa, sfin_s = _gla(gq, gk, gv, la, gg, gnorm, state_gla, l, bs, ls, ls, math.gcd(ls, GLA_CHUNK),
                          F32, F32)
        ob = _attn_sample(page_table, qh, ck, cv, l, kf, vf, dg, lam4, dnorm, ls, SAMPLE_PAGES_PER_STEP,
                          lam_init)
        xs = _merge(xs, oa, ob, ga, gb, wpa, wpb, wo, gpost, bs * ls)
        outs[3].append(kf.reshape(bs, ls, DIFF_KV_HEADS, 2 * DIFF_DH))
        outs[4].append(vf.reshape(bs, ls, DIFF_KV_HEADS, 2 * DIFF_DH))
        outs[5].append(sfin_s)

    cache_shape = (depth, bp, lp, DIFF_KV_HEADS, 2 * DIFF_DH)
    return (xp.reshape(bp, lp, D_MODEL), xs.reshape(bs, ls, D_MODEL),
            kv_prompt[0].reshape(cache_shape), kv_prompt[1].reshape(cache_shape), jnp.stack(outs[2]),
            jnp.stack(outs[3]), jnp.stack(outs[4]), jnp.stack(outs[5]))
```
